```python
import math
import jax, jax.numpy as jnp
from jax import lax
import numpy as np

D_MODEL = 1024
BATCH = 8
SEQ = 8192
DEPTH = 4

HEAD_DIM = 64
NSA_HEADS = 12
NSA_KV_HEADS = 4
NSA_GROUP = NSA_HEADS // NSA_KV_HEADS
NSA_WIDTH = NSA_HEADS * HEAD_DIM
NSA_KV_WIDTH = NSA_KV_HEADS * HEAD_DIM
N_BRANCH = 3
CMP_LEN = 32
CMP_STRIDE = 16
CMP_HIDDEN = 4 * HEAD_DIM
SEL_BLOCK = 64
SEL_TOP = 16
WINDOW = 512
Q_BLOCK = 128
GMLP_CHUNK = 128
GMLP_GROUPS = 12
GMLP_WIDTH = 3 * D_MODEL // 4
GMLP_GROUP_DIM = GMLP_WIDTH // GMLP_GROUPS
MEM_LEN = 256
MEM_HEADS = 4
MEM_WIDTH = MEM_HEADS * HEAD_DIM
D_FF = -(-8 * D_MODEL // (3 * 256)) * 256
ROPE_THETA = 10000.0
NORM_EPS = 1e-6
NEG_INF = -1e30
FORCE_BONUS = 1e4
NSA_IN = NSA_WIDTH + 6 * NSA_KV_WIDTH + N_BRANCH * NSA_HEADS + MEM_WIDTH
GMLP_IN = 2 * GMLP_WIDTH + MEM_WIDTH

kernel_name = 'hybrid_gmlp_nsa_interleaved'


def rmsnorm(x, g):
    xf = x.astype(jnp.float32)
    y = xf * lax.rsqrt(jnp.mean(xf * xf, axis=-1, keepdims=True) + NORM_EPS)
    return y.astype(x.dtype) * g


def rope(x):
    s, d = x.shape[1], x.shape[-1]
    half = d // 2
    pos = jnp.arange(s, dtype=jnp.float32)
    inv = ROPE_THETA ** (-jnp.arange(half, dtype=jnp.float32) / half)
    ang = pos[:, None] * inv[None, :]
    cos = jnp.cos(ang)[None, :, None, :]
    sin = jnp.sin(ang)[None, :, None, :]
    xf = x.astype(jnp.float32)
    x1, x2 = xf[..., :half], xf[..., half:]
    return jnp.concatenate([x1 * cos - x2 * sin, x2 * cos + x1 * sin], axis=-1).astype(x.dtype)


def swiglu(h, wg, wu, wd):
    return (jax.nn.silu(h @ wg) * (h @ wu)) @ wd


def mem_cross_attention(q, mem_n, w_kv):
    b, s, _ = q.shape
    m = mem_n.shape[1]
    q = q.reshape(b, s, MEM_HEADS, HEAD_DIM)
    k, v = jnp.split(mem_n @ w_kv, 2, axis=-1)
    k = k.reshape(b, m, MEM_HEADS, HEAD_DIM)
    v = v.reshape(b, m, MEM_HEADS, HEAD_DIM)
    sc = jnp.einsum('bshd,bmhd->bhsm', q, k).astype(jnp.float32) * (HEAD_DIM ** -0.5)
    p = jax.nn.softmax(sc, axis=-1).astype(v.dtype)
    return jnp.einsum('bhsm,bmhd->bshd', p, v).reshape(b, s, MEM_WIDTH)


def gmlp_mixer(h, w_in, v_gain, w_s, b_s):
    b, s, _ = h.shape
    proj = h @ w_in
    z, mem_q = proj[..., :2 * GMLP_WIDTH], proj[..., 2 * GMLP_WIDTH:]
    u, v = jnp.split(jax.nn.gelu(z), 2, axis=-1)
    nc = s // GMLP_CHUNK
    vn = rmsnorm(v, v_gain).reshape(b, nc, GMLP_CHUNK, GMLP_GROUPS, GMLP_GROUP_DIM)
    causal = jnp.tril(jnp.ones((GMLP_CHUNK, GMLP_CHUNK), dtype=bool))
    w = jnp.where(causal[None], w_s, jnp.zeros((), w_s.dtype))
    mixed = jnp.einsum('gts,bcsgd->bctgd', w, vn) + b_s.T[None, None, :, :, None]
    return u * mixed.reshape(b, s, GMLP_WIDTH), mem_q


def compress_blocks(k, pe, w1, w2):
    b, s, g, d = k.shape
    kc = k.reshape(b, s // CMP_STRIDE, CMP_STRIDE, g, d)
    blocks = jnp.concatenate([kc[:, :-1], kc[:, 1:]], axis=2) + pe[None, None, :, None, :]
    flat = blocks.transpose(0, 1, 3, 2, 4).reshape(b, s // CMP_STRIDE - 1, g, CMP_LEN * d)
    return jax.nn.silu(flat @ w1) @ w2


def selection_map(s):
    nc = s // CMP_STRIDE - 1
    nb = s // SEL_BLOCK
    c0 = jnp.arange(nc) * CMP_STRIDE
    b0 = jnp.arange(nb) * SEL_BLOCK
    lo = jnp.maximum(c0[:, None], b0[None, :])
    hi = jnp.minimum(c0[:, None] + CMP_LEN, b0[None, :] + SEL_BLOCK)
    return jnp.clip(hi - lo, 0).astype(jnp.float32) / CMP_LEN


def nsa_single(q, kc, vc, ks, vs, kw, vw, gates, sel_map):
    s = q.shape[0]
    nc = kc.shape[0]
    nb = s // SEL_BLOCK
    n_sel = min(SEL_TOP, nb)
    scale = HEAD_DIM ** -0.5
    qg = q.reshape(s, NSA_KV_HEADS, NSA_GROUP, HEAD_DIM)
    gg = gates.reshape(s, NSA_KV_HEADS, NSA_GROUP, N_BRANCH)
    ks_blk = ks.reshape(nb, SEL_BLOCK, NSA_KV_HEADS, HEAD_DIM).transpose(2, 0, 1, 3)
    vs_blk = vs.reshape(nb, SEL_BLOCK, NSA_KV_HEADS, HEAD_DIM).transpose(2, 0, 1, 3)
    kw_pad = jnp.pad(kw, ((WINDOW, 0), (0, 0), (0, 0)))
    vw_pad = jnp.pad(vw, ((WINDOW, 0), (0, 0), (0, 0)))
    cmp_end = jnp.arange(nc) * CMP_STRIDE + (CMP_LEN - 1)
    blk_ids = jnp.arange(nb)
    in_blk = jnp.arange(SEL_BLOCK)
    win_off = jnp.arange(Q_BLOCK + WINDOW) - WINDOW
    gather = jax.vmap(lambda kb, ib: kb[ib], in_axes=(0, 1), out_axes=1)

    def block(i):
        t0 = i * Q_BLOCK
        tq = t0 + jnp.arange(Q_BLOCK)
        qb = lax.dynamic_slice_in_dim(qg, t0, Q_BLOCK, 0)
        gb = lax.dynamic_slice_in_dim(gg, t0, Q_BLOCK, 0)
        ok_c = cmp_end[None, :] <= tq[:, None]
        sc = jnp.einsum('qgrd,cgd->qgrc', qb, kc).astype(jnp.float32) * scale
        sc = jnp.where(ok_c[:, None, None, :], sc, NEG_INF)
        p_c = jax.nn.softmax(sc, axis=-1) * jnp.any(ok_c, axis=-1).astype(jnp.float32)[:, None, None, None]
        o_c = jnp.einsum('qgrc,cgd->qgrd', p_c.astype(vc.dtype), vc)
        imp = jnp.einsum('qgrc,cn->qgn', p_c, sel_map)
        cur = tq // SEL_BLOCK
        forced = (blk_ids[None, :] == 0) | (blk_ids[None, :] == cur[:, None]) | (blk_ids[None, :] == cur[:, None] - 1)
        imp = imp + jnp.where(forced, FORCE_BONUS, 0.0)[:, None, :]
        imp = jnp.where((blk_ids[None, :] <= cur[:, None])[:, None, :], imp, NEG_INF)
        _, idx = lax.top_k(imp, n_sel)
        k_sel = gather(ks_blk, idx).reshape(Q_BLOCK, NSA_KV_HEADS, n_sel * SEL_BLOCK, HEAD_DIM)
        v_sel = gather(vs_blk, idx).reshape(Q_BLOCK, NSA_KV_HEADS, n_sel * SEL_BLOCK, HEAD_DIM)
        pos_sel = (idx[..., None] * SEL_BLOCK + in_blk).reshape(Q_BLOCK, NSA_KV_HEADS, n_sel * SEL_BLOCK)
        ok_s = pos_sel <= tq[:, None, None]
        ss = jnp.einsum('qgrd,qgmd->qgrm', qb, k_sel).astype(jnp.float32) * scale
        ss = jnp.where(ok_s[:, :, None, :], ss, NEG_INF)
        o_s = jnp.einsum('qgrm,qgmd->qgrd', jax.nn.softmax(ss, axis=-1).astype(v_sel.dtype), v_sel)
        k_win = lax.dynamic_slice_in_dim(kw_pad, t0, Q_BLOCK + WINDOW, 0)
        v_win = lax.dynamic_slice_in_dim(vw_pad, t0, Q_BLOCK + WINDOW, 0)
        pos_w = t0 + win_off
        ok_w = (pos_w[None, :] >= 0) & (pos_w[None, :] <= tq[:, None]) & (pos_w[None, :] > tq[:, None] - WINDOW)
        sw = jnp.einsum('qgrd,kgd->qgrk', qb, k_win).astype(jnp.float32) * scale
        sw = jnp.where(ok_w[:, None, None, :], sw, NEG_INF)
        o_w = jnp.einsum('qgrk,kgd->qgrd', jax.nn.softmax(sw, axis=-1).astype(v_win.dtype), v_win)
        o = gb[..., 0:1] * o_c + gb[..., 1:2] * o_s + gb[..., 2:3] * o_w
        return o.reshape(Q_BLOCK, NSA_WIDTH)

    out = lax.map(block, jnp.arange(s // Q_BLOCK))
    return out.reshape(s, NSA_WIDTH)


def nsa_mixer(h, w_in, pe_k, pe_v, ck_w1, ck_w2, cv_w1, cv_w2):
    b, s, _ = h.shape
    sizes = [NSA_WIDTH] + [NSA_KV_WIDTH] * 6 + [N_BRANCH * NSA_HEADS, MEM_WIDTH]
    offsets = [int(o) for o in np.cumsum(sizes)[:-1]]
    q, k_cmp, v_cmp, k_slc, v_slc, k_win, v_win, g, mem_q = jnp.split(h @ w_in, offsets, axis=-1)
    heads = lambda t, n: t.reshape(b, s, n, HEAD_DIM)
    q = rope(heads(q, NSA_HEADS))
    kc = compress_blocks(rope(heads(k_cmp, NSA_KV_HEADS)), pe_k, ck_w1, ck_w2)
    vc = compress_blocks(heads(v_cmp, NSA_KV_HEADS), pe_v, cv_w1, cv_w2)
    ks = rope(heads(k_slc, NSA_KV_HEADS))
    vs = heads(v_slc, NSA_KV_HEADS)
    kw = rope(heads(k_win, NSA_KV_HEADS))
    vw = heads(v_win, NSA_KV_HEADS)
    gates = jax.nn.sigmoid(g.astype(jnp.float32)).astype(h.dtype).reshape(b, s, NSA_HEADS, N_BRANCH)
    sel_map = selection_map(s)
    out = lax.map(lambda a: nsa_single(a[0], a[1], a[2], a[3], a[4], a[5], a[6], a[7], sel_map),
                  (q, kc, vc, ks, vs, kw, vw, gates))
    return out, mem_q


def setup_inputs(seed: int = 0) -> dict:
    key = jax.random.key(seed)
    k = jax.random.split(key, 23)
    n_a = (DEPTH + 1) // 2
    n_b = DEPTH // 2
    nrm = lambda kk, shape, sc: jax.random.normal(kk, shape, jnp.float32) * sc
    out_sc = (2.0 * DEPTH) ** -0.5
    return {
        'x': nrm(k[0], (BATCH, SEQ, D_MODEL), 1.0),
        'mem': nrm(k[1], (BATCH, MEM_LEN, D_MODEL), 1.0),
        'norm_mix': 1.0 + nrm(k[2], (DEPTH, D_MODEL), 0.02),
        'norm_ffn': 1.0 + nrm(k[3], (DEPTH, D_MODEL), 0.02),
        'norm_mem': 1.0 + nrm(k[4], (DEPTH, D_MODEL), 0.02),
        'norm_final': 1.0 + nrm(k[5], (D_MODEL,), 0.02),
        'w_mem_kv': nrm(k[6], (DEPTH, D_MODEL, 2 * MEM_WIDTH), D_MODEL ** -0.5),
        'ffn_w_gate': nrm(k[7], (DEPTH, D_MODEL, D_FF), D_MODEL ** -0.5),
        'ffn_w_up': nrm(k[8], (DEPTH, D_MODEL, D_FF), D_MODEL ** -0.5),
        'ffn_w_down': nrm(k[9], (DEPTH, D_FF, D_MODEL), D_FF ** -0.5 * out_sc),
        'gmlp_w_in': nrm(k[10], (n_a, D_MODEL, GMLP_IN), D_MODEL ** -0.5),
        'gmlp_v_norm': 1.0 + nrm(k[11], (n_a, GMLP_WIDTH), 0.02),
        'gmlp_w_s': nrm(k[12], (n_a, GMLP_GROUPS, GMLP_CHUNK, GMLP_CHUNK), GMLP_CHUNK ** -0.5),
        'gmlp_b_s': 1.0 + nrm(k[13], (n_a, GMLP_GROUPS, GMLP_CHUNK), 0.02),
        'gmlp_w_out': nrm(k[14], (n_a, GMLP_WIDTH + MEM_WIDTH, D_MODEL), (GMLP_WIDTH + MEM_WIDTH) ** -0.5 * out_sc),
        'nsa_w_in': nrm(k[15], (n_b, D_MODEL, NSA_IN), D_MODEL ** -0.5),
        'nsa_pe_k': nrm(k[16], (n_b, CMP_LEN, HEAD_DIM), 0.1),
        'nsa_pe_v': nrm(k[17], (n_b, CMP_LEN, HEAD_DIM), 0.1),
        'nsa_ck_w1': nrm(k[18], (n_b, CMP_LEN * HEAD_DIM, CMP_HIDDEN), (CMP_LEN * HEAD_DIM) ** -0.5),
        'nsa_ck_w2': nrm(k[19], (n_b, CMP_HIDDEN, HEAD_DIM), CMP_HIDDEN ** -0.5),
        'nsa_cv_w1': nrm(k[20], (n_b, CMP_LEN * HEAD_DIM, CMP_HIDDEN), (CMP_LEN * HEAD_DIM) ** -0.5),
        'nsa_cv_w2': nrm(k[21], (n_b, CMP_HIDDEN, HEAD_DIM), CMP_HIDDEN ** -0.5),
        'nsa_w_out': nrm(k[22], (n_b, NSA_WIDTH + MEM_WIDTH, D_MODEL), (NSA_WIDTH + MEM_WIDTH) ** -0.5 * out_sc),
    }


def reference(x, mem, norm_mix, norm_ffn, norm_mem, norm_final, w_mem_kv, ffn_w_gate, ffn_w_up, ffn_w_down,
              gmlp_w_in, gmlp_v_norm, gmlp_w_s, gmlp_b_s, gmlp_w_out,
              nsa_w_in, nsa_pe_k, nsa_pe_v, nsa_ck_w1, nsa_ck_w2, nsa_cv_w1, nsa_cv_w2, nsa_w_out):
    for i in range(DEPTH):
        j = i // 2
        h = rmsnorm(x, norm_mix[i])
        mem_n = rmsnorm(mem, norm_mem[i])
        if i % 2 == 0:
            mix, mem_q = gmlp_mixer(h, gmlp_w_in[j], gmlp_v_norm[j], gmlp_w_s[j], gmlp_b_s[j])
            w_out = gmlp_w_out[j]
        else:
            mix, mem_q = nsa_mixer(h, nsa_w_in[j], nsa_pe_k[j], nsa_pe_v[j], nsa_ck_w1[j], nsa_ck_w2[j],
                                   nsa_cv_w1[j], nsa_cv_w2[j])
            w_out = nsa_w_out[j]
        mem_o = mem_cross_attention(mem_q, mem_n, w_mem_kv[i])
        x = x + jnp.concatenate([mix, mem_o], axis=-1) @ w_out
        h = rmsnorm(x, norm_ffn[i])
        x = x + swiglu(h, ffn_w_gate[i], ffn_w_up[i], ffn_w_down[i])
    return rmsnorm(x, norm_final)
```

```python
import functools

import jax
import jax.numpy as jnp
from jax import lax
from jax.experimental import pallas as pl
from jax.experimental.pallas import tpu as pltpu

F32 = jnp.float32
BF16 = jnp.bfloat16

D_MODEL = 1024
HEAD_DIM = 64
NSA_HEADS = 12
NSA_KV_HEADS = 4
NSA_GROUP = NSA_HEADS // NSA_KV_HEADS
NSA_WIDTH = NSA_HEADS * HEAD_DIM
NSA_KV_WIDTH = NSA_KV_HEADS * HEAD_DIM
N_BRANCH = 3
CMP_LEN = 32
CMP_STRIDE = 16
CMP_HIDDEN = 4 * HEAD_DIM
SEL_BLOCK = 64
SEL_TOP = 16
WINDOW = 512
Q_BLOCK = 128
GMLP_CHUNK = 128
GMLP_GROUPS = 12
GMLP_WIDTH = 3 * D_MODEL // 4
MEM_HEADS = 4
MEM_WIDTH = MEM_HEADS * HEAD_DIM
D_FF = 2816
ROPE_THETA = 10000.0
NORM_EPS = 1e-6
NEG_INF = -1e30
SCALE = HEAD_DIM ** -0.5

LANES = 128
ROW_TILE = 512
KEY_TILE = 512
FF_CHUNK = 256
Q_COLS = NSA_GROUP * Q_BLOCK
FORCED_KEY = 0x7F000000
VMEM_LIMIT = 56 * 1024 * 1024


def _const_spec(shape):
    zeros = (0,) * len(shape)
    return pl.BlockSpec(shape, lambda *_: zeros, pipeline_mode=pl.Buffered(1))


def _params(n_axes):
    return pltpu.CompilerParams(
        dimension_semantics=("arbitrary",) * n_axes, vmem_limit_bytes=VMEM_LIMIT)


def _rms(xf, gain):
    ms = jnp.mean(xf * xf, axis=-1, keepdims=True)
    return (xf * lax.rsqrt(ms + NORM_EPS)) * gain


def _dot(a, b):
    return jnp.dot(a, b, preferred_element_type=F32)


def _dot_nt(a, b):
    return lax.dot_general(a, b, (((1,), (1,)), ((), ())), preferred_element_type=F32)


def _dot_tn(a, b):
    return lax.dot_general(a, b, (((0,), (0,)), ((), ())), preferred_element_type=F32)


def _sigmoid(x):
    return 1.0 / (1.0 + jnp.exp(-x))


def _gelu_tanh(x):
    c = 0.7978845608028654
    return 0.5 * x * (1.0 + jnp.tanh(c * (x + 0.044715 * (x * x * x))))


def _norm_matmul_kernel(x_ref, g_ref, w_ref, o_ref):
    h = _rms(x_ref[...], g_ref[...]).astype(BF16)
    o_ref[...] = _dot(h, w_ref[...]).astype(o_ref.dtype)


def _norm_matmul(x, gain, w, out_dtype):
    rows, d = x.shape
    n = w.shape[1]
    tm = min(ROW_TILE, rows)
    return pl.pallas_call(
        _norm_matmul_kernel,
        grid=(rows // tm,),
        in_specs=[pl.BlockSpec((tm, d), lambda i: (i, 0)), _const_spec((1, d)), _const_spec((d, n))],
        out_specs=pl.BlockSpec((tm, n), lambda i: (i, 0)),
        out_shape=jax.ShapeDtypeStruct((rows, n), out_dtype),
        compiler_params=_params(1),
        name="mem_kv",
    )(x, gain, w)


def _gmlp_kernel(x_ref, g_ref, win_ref, vg_ref, ws_ref, bs_ref, mix_ref, mq_ref):
    tm = x_ref.shape[0]
    h = _rms(x_ref[...], g_ref[...]).astype(BF16)
    mq_ref[...] = _dot(h, win_ref[:, 2 * GMLP_WIDTH:]).astype(BF16)
    u = _gelu_tanh(_dot(h, win_ref[:, :GMLP_WIDTH]))
    v = _gelu_tanh(_dot(h, win_ref[:, GMLP_WIDTH:2 * GMLP_WIDTH]))
    vn = _rms(v, vg_ref[...]).astype(BF16)

    row = lax.broadcasted_iota(jnp.int32, (GMLP_CHUNK, GMLP_CHUNK), 0)
    col = lax.broadcasted_iota(jnp.int32, (GMLP_CHUNK, GMLP_CHUNK), 1)
    causal = col <= row
    first_group = lax.broadcasted_iota(jnp.int32, (GMLP_CHUNK, LANES), 1) < (LANES // 2)
    zero = jnp.zeros((), BF16)
    for slab in range(GMLP_WIDTH // LANES):
        w0 = jnp.where(causal, ws_ref[2 * slab], zero)
        w1 = jnp.where(causal, ws_ref[2 * slab + 1], zero)
        cols = slice(slab * LANES, (slab + 1) * LANES)
        bias = bs_ref[:, cols]
        for c in range(tm // GMLP_CHUNK):
            rows = slice(c * GMLP_CHUNK, (c + 1) * GMLP_CHUNK)
            vv = vn[rows, cols]
            mixed = jnp.where(first_group, _dot(w0, vv), _dot(w1, vv)) + bias
            mix_ref[rows, cols] = (u[rows, cols] * mixed).astype(BF16)


def _gmlp_mixer(x, gain, w_in, v_gain, w_s, b_full):
    rows, d = x.shape
    tm = ROW_TILE
    return pl.pallas_call(
        _gmlp_kernel,
        grid=(rows // tm,),
        in_specs=[
            pl.BlockSpec((tm, d), lambda i: (i, 0)),
            _const_spec((1, d)),
            _const_spec(w_in.shape),
            _const_spec((1, GMLP_WIDTH)),
            _const_spec(w_s.shape),
            _const_spec(b_full.shape),
        ],
        out_specs=[
            pl.BlockSpec((tm, GMLP_WIDTH), lambda i: (i, 0)),
            pl.BlockSpec((tm, MEM_WIDTH), lambda i: (i, 0)),
        ],
        out_shape=[
            jax.ShapeDtypeStruct((rows, GMLP_WIDTH), BF16),
            jax.ShapeDtypeStruct((rows, MEM_WIDTH), BF16),
        ],
        compiler_params=_params(1),
        name="gmlp_mixer",
    )(x, gain, w_in, v_gain, w_s, b_full)


_Q0 = 0
_QR0 = _Q0 + NSA_WIDTH
_KC0 = _QR0 + NSA_WIDTH
_KS0 = _KC0 + 2 * NSA_KV_WIDTH
_KW0 = _KS0 + 2 * NSA_KV_WIDTH
_VC0 = _KW0 + 2 * NSA_KV_WIDTH
_VS0 = _VC0 + NSA_KV_WIDTH
_VW0 = _VS0 + NSA_KV_WIDTH
_MQ0 = _VW0 + NSA_KV_WIDTH
_G0 = _MQ0 + MEM_WIDTH
_NSA_PACKED = _G0 + LANES


def _nsa_in_kernel(x_ref, g_ref, w_ref, cos_ref, sin_ref,
                   q_ref, kc_ref, vc_ref, ks_ref, vs_ref, kw_ref, vw_ref, mq_ref, gate_ref):
    h = _rms(x_ref[...], g_ref[...]).astype(BF16)
    cos = cos_ref[...]
    sin = sin_ref[...]

    def proj(c0, width):
        return _dot(h, w_ref[:, c0:c0 + width])

    def roped(c0, width):
        reps = width // LANES
        cw = jnp.concatenate([cos] * reps, axis=1)
        sw = jnp.concatenate([sin] * reps, axis=1)
        return proj(c0, width) * cw + proj(c0 + width, width) * sw

    q_ref[...] = (roped(_Q0, NSA_WIDTH) * SCALE).astype(BF16)
    kc_ref[...] = roped(_KC0, NSA_KV_WIDTH)
    ks_ref[...] = roped(_KS0, NSA_KV_WIDTH).astype(BF16)
    kw_ref[...] = roped(_KW0, NSA_KV_WIDTH).astype(BF16)
    vc_ref[...] = proj(_VC0, NSA_KV_WIDTH)
    vs_ref[...] = proj(_VS0, NSA_KV_WIDTH).astype(BF16)
    vw_ref[...] = proj(_VW0, NSA_KV_WIDTH).astype(BF16)
    mq_ref[...] = proj(_MQ0, MEM_WIDTH).astype(BF16)
    gate_ref[...] = _sigmoid(proj(_G0, LANES))


def _nsa_in_proj(x, gain, w_packed, cos, sin, seq):
    rows, d = x.shape
    tm = ROW_TILE
    tiles_per_seq = seq // tm
    row_spec = lambda width: pl.BlockSpec((tm, width), lambda i: (i, 0))
    tab_spec = pl.BlockSpec((tm, LANES), lambda i: (i % tiles_per_seq, 0))
    widths = [NSA_WIDTH] + [NSA_KV_WIDTH] * 6 + [MEM_WIDTH, LANES]
    dtypes = [BF16, F32, F32, BF16, BF16, BF16, BF16, BF16, F32]
    return pl.pallas_call(
        _nsa_in_kernel,
        grid=(rows // tm,),
        in_specs=[row_spec(d), _const_spec((1, d)), _const_spec(w_packed.shape), tab_spec, tab_spec],
        out_specs=[row_spec(w) for w in widths],
        out_shape=[jax.ShapeDtypeStruct((rows, w), dt) for w, dt in zip(widths, dtypes)],
        compiler_params=_params(1),
        name="nsa_in_proj",
    )(x, gain, w_packed, cos, sin)


def _compress_kernel(xk_ref, xv_ref, pek_ref, pev_ref, k1a_ref, k1b_ref, k2_ref,
                     v1a_ref, v1b_ref, v2_ref, ok_ref, ov_ref):
    n = xk_ref.shape[2]
    keep = lax.broadcasted_iota(jnp.int32, (n, HEAD_DIM), 0) < (n - 1)

    def one(x_ref, pe_ref, w1a_ref, w1b_ref, w2_ref, o_ref):
        x = x_ref[0, 0]
        first = _dot((x + pe_ref[0:1, :]).astype(BF16), w1a_ref[...])
        second = _dot((x + pe_ref[1:2, :]).astype(BF16), w1b_ref[...])
        pre = first + pltpu.roll(second, n - 1, 0)
        hidden = (pre * _sigmoid(pre)).astype(BF16)
        out = _dot(hidden, w2_ref[...])
        o_ref[0, 0] = jnp.where(keep, out, 0.0).astype(BF16)

    one(xk_ref, pek_ref, k1a_ref, k1b_ref, k2_ref, ok_ref)
    one(xv_ref, pev_ref, v1a_ref, v1b_ref, v2_ref, ov_ref)


def _compress(xk, xv, pek, pev, k1a, k1b, k2, v1a, v1b, v2):
    b, g, n, w = xk.shape
    x_spec = pl.BlockSpec((1, 1, n, w), lambda i, j: (i, j, 0, 0))
    o_spec = pl.BlockSpec((1, 1, n, HEAD_DIM), lambda i, j: (i, j, 0, 0))
    consts = [pek, pev, k1a, k1b, k2, v1a, v1b, v2]
    return pl.pallas_call(
        _compress_kernel,
        grid=(b, g),
        in_specs=[x_spec, x_spec] + [_const_spec(c.shape) for c in consts],
        out_specs=[o_spec, o_spec],
        out_shape=[jax.ShapeDtypeStruct((b, g, n, HEAD_DIM), BF16)] * 2,
        compiler_params=_params(2),
        name="nsa_compress",
    )(xk, xv, *consts)


def _softmax_cols(s):
    m = jnp.max(s, axis=0, keepdims=True)
    e = jnp.exp(s - m)
    return e, 1.0 / jnp.sum(e, axis=0, keepdims=True)


def _tile3(a):
    return jnp.concatenate([a] * NSA_GROUP, axis=1)


def _nsa_attn_kernel(q_ref, kc_ref, vc_ref, ks_ref, vs_ref, kw_ref, vw_ref, gate_ref,
                     selt_ref, ltri_ref, o_ref, mask_ref):
    i = pl.program_id(2)
    n_cmp = kc_ref.shape[2]
    n_blk = selt_ref.shape[0]
    q = q_ref[0].reshape(Q_COLS, HEAD_DIM)

    c_idx = lax.broadcasted_iota(jnp.int32, (n_cmp, Q_BLOCK), 0)
    t_idx = lax.broadcasted_iota(jnp.int32, (n_cmp, Q_BLOCK), 1) + i * Q_BLOCK
    ok_c = _tile3(c_idx * CMP_STRIDE + (CMP_LEN - 1) <= t_idx)
    s_c = jnp.where(ok_c, _dot_nt(kc_ref[0, 0], q), NEG_INF)
    e_c, inv_c = _softmax_cols(s_c)
    t_row = lax.broadcasted_iota(jnp.int32, (1, Q_BLOCK), 1) + i * Q_BLOCK
    any_c = _tile3((t_row >= CMP_LEN - 1).astype(F32))
    p_c = e_c * (inv_c * any_c)
    o_cmp = _dot_tn(vc_ref[0, 0], p_c.astype(BF16))

    p_sum = p_c[:, :Q_BLOCK] + p_c[:, Q_BLOCK:2 * Q_BLOCK] + p_c[:, 2 * Q_BLOCK:]
    p_hi = p_sum.astype(BF16)
    p_lo = (p_sum - p_hi.astype(F32)).astype(BF16)
    imp = _dot(selt_ref[...], p_hi) + _dot(selt_ref[...], p_lo)

    blk = lax.broadcasted_iota(jnp.int32, (n_blk, Q_BLOCK), 0)
    tok = lax.broadcasted_iota(jnp.int32, (n_blk, Q_BLOCK), 1)
    cur = 2 * i + (tok >= SEL_BLOCK).astype(jnp.int32)
    forced = (blk == 0) | (blk == cur) | (blk == cur - 1)
    key = lax.bitcast_convert_type(jnp.maximum(imp, 0.0), jnp.int32)
    key = jnp.where(forced, FORCED_KEY, key)
    key = jnp.where(blk <= cur, key, -1)
    top = float(min(SEL_TOP, n_blk))

    def search(it, thr):
        cand = thr | lax.shift_left(jnp.int32(1), 30 - it)
        cnt = jnp.sum(jnp.where(key >= cand, 1.0, 0.0), axis=0, keepdims=True)
        return jnp.where(cnt >= top, cand, thr)

    thr = lax.fori_loop(0, 31, search, jnp.zeros((1, Q_BLOCK), jnp.int32))
    above = key > thr
    tied = key == thr
    need = top - jnp.sum(jnp.where(above, 1.0, 0.0), axis=0, keepdims=True)
    before = _dot(ltri_ref[...], jnp.where(tied, 1.0, 0.0).astype(BF16))
    mask_ref[...] = jnp.where(above | (tied & (before < need)), 1.0, 0.0)

    def sel_tile(j, carry, causal):
        m_run, l_run, acc = carry
        k0 = pl.multiple_of(j * KEY_TILE, KEY_TILE)
        s = _dot_nt(ks_ref[0, 0, pl.ds(k0, KEY_TILE), :], q)
        b0 = pl.multiple_of(j * (KEY_TILE // SEL_BLOCK), KEY_TILE // SEL_BLOCK)
        rows = mask_ref[pl.ds(b0, KEY_TILE // SEL_BLOCK), :]
        picked = jnp.concatenate(
            [jnp.broadcast_to(rows[r:r + 1, :], (SEL_BLOCK, Q_BLOCK)) for r in range(KEY_TILE // SEL_BLOCK)],
            axis=0) > 0.5
        if causal:
            pos = lax.broadcasted_iota(jnp.int32, (KEY_TILE, Q_BLOCK), 0) + j * KEY_TILE
            tq = lax.broadcasted_iota(jnp.int32, (KEY_TILE, Q_BLOCK), 1) + i * Q_BLOCK
            picked = picked & (pos <= tq)
        s = jnp.where(_tile3(picked), s, NEG_INF)
        m_new = jnp.maximum(m_run, jnp.max(s, axis=0, keepdims=True))
        alpha = jnp.exp(m_run - m_new)
        p = jnp.exp(s - m_new)
        l_new = alpha * l_run + jnp.sum(p, axis=0, keepdims=True)
        pv = _dot_tn(vs_ref[0, 0, pl.ds(k0, KEY_TILE), :], p.astype(BF16))
        return m_new, l_new, alpha * acc + pv

    init = (jnp.full((1, Q_COLS), NEG_INF, F32), jnp.zeros((1, Q_COLS), F32),
            jnp.zeros((HEAD_DIM, Q_COLS), F32))
    last = (i * Q_BLOCK) // KEY_TILE
    carry = lax.fori_loop(0, last, lambda j, c: sel_tile(j, c, False), init)
    _, l_sel, acc_sel = sel_tile(last, carry, True)
    o_sel = acc_sel * (1.0 / l_sel)

    span = WINDOW + Q_BLOCK
    w0 = pl.multiple_of(i * Q_BLOCK, Q_BLOCK)
    k_loc = lax.broadcasted_iota(jnp.int32, (span, Q_BLOCK), 0)
    t_loc = lax.broadcasted_iota(jnp.int32, (span, Q_BLOCK), 1)
    ok_w = (k_loc > t_loc) & (k_loc <= t_loc + WINDOW) & (k_loc >= WINDOW - i * Q_BLOCK)
    s_w = jnp.where(_tile3(ok_w), _dot_nt(kw_ref[0, 0, pl.ds(w0, span), :], q), NEG_INF)
    e_w, inv_w = _softmax_cols(s_w)
    o_win = _dot_tn(vw_ref[0, 0, pl.ds(w0, span), :], e_w.astype(BF16)) * inv_w

    gates = gate_ref[0, 0, 0]
    o_t = gates[0:1, :] * o_cmp + gates[1:2, :] * o_sel + gates[2:3, :] * o_win
    o_ref[0] = o_t.T.reshape(NSA_GROUP, Q_BLOCK, HEAD_DIM).astype(BF16)


def _nsa_attention(q, kc, vc, ks, vs, kw, vw, gates, sel_t, ltri):
    b, _, seq, _ = q.shape
    n_cmp = kc.shape[2]
    n_blk = seq // SEL_BLOCK
    n_q = seq // Q_BLOCK
    per_group = lambda rows: pl.BlockSpec((1, 1, rows, HEAD_DIM), lambda bi, gi, qi: (bi, gi, 0, 0))
    q_spec = pl.BlockSpec((1, NSA_GROUP, Q_BLOCK, HEAD_DIM), lambda bi, gi, qi: (bi, gi, qi, 0))
    return pl.pallas_call(
        _nsa_attn_kernel,
        grid=(b, NSA_KV_HEADS, n_q),
        in_specs=[
            q_spec,
            per_group(n_cmp), per_group(n_cmp),
            per_group(seq), per_group(seq),
            per_group(seq + WINDOW), per_group(seq + WINDOW),
            pl.BlockSpec((1, 1, 1, N_BRANCH, Q_COLS), lambda bi, gi, qi: (bi, gi, qi, 0, 0)),
            _const_spec(sel_t.shape),
            _const_spec(ltri.shape),
        ],
        out_specs=q_spec,
        out_shape=jax.ShapeDtypeStruct(q.shape, BF16),
        scratch_shapes=[pltpu.VMEM((n_blk, Q_BLOCK), F32)],
        compiler_params=_params(3),
        name="nsa_attention",
    )(q, kc, vc, ks, vs, kw, vw, gates, sel_t, ltri)


def _post_kernel(x_ref, mix_ref, mq_ref, mkv_ref, wo_ref, gf_ref, wg_ref, wu_ref, wd_ref, gl_ref,
                 o_ref, act_ref, *, final):
    tm = x_ref.shape[0]
    mq = mq_ref[...]
    mk = mkv_ref[0, :, :MEM_WIDTH]
    mv = mkv_ref[0, :, MEM_WIDTH:]
    lane = lax.broadcasted_iota(jnp.int32, (tm, MEM_WIDTH), 1)
    mem_o = jnp.zeros((tm, MEM_WIDTH), F32)
    zero = jnp.zeros((), BF16)
    for hd in range(MEM_HEADS):
        in_head = (lane >= hd * HEAD_DIM) & (lane < (hd + 1) * HEAD_DIM)
        s = _dot_nt(jnp.where(in_head, mq, zero), mk) * SCALE
        m = jnp.max(s, axis=-1, keepdims=True)
        e = jnp.exp(s - m)
        p = e * (1.0 / jnp.sum(e, axis=-1, keepdims=True))
        mem_o = jnp.where(in_head, _dot(p.astype(BF16), mv), mem_o)
    mix_w = mix_ref.shape[1]
    y = _dot(mix_ref[...], wo_ref[:mix_w, :]) + _dot(mem_o.astype(BF16), wo_ref[mix_w:, :])
    x1 = x_ref[...] + y

    h = _rms(x1, gf_ref[...]).astype(BF16)
    for c in range(D_FF // FF_CHUNK):
        cols = slice(c * FF_CHUNK, (c + 1) * FF_CHUNK)
        gate = _dot(h, wg_ref[:, cols])
        up = _dot(h, wu_ref[:, cols])
        act_ref[:, cols] = (gate * _sigmoid(gate) * up).astype(BF16)
    x2 = x1 + _dot(act_ref[...], wd_ref[...])
    if final:
        x2 = _rms(x2, gl_ref[...])
    o_ref[...] = x2


def _post(x, mix, mem_q, mem_kv, w_out, g_ffn, w_gate, w_up, w_down, g_last, seq, final):
    rows, d = x.shape
    tm = ROW_TILE
    tiles_per_seq = seq // tm
    row_spec = lambda width: pl.BlockSpec((tm, width), lambda i: (i, 0))
    return pl.pallas_call(
        functools.partial(_post_kernel, final=final),
        grid=(rows // tm,),
        in_specs=[
            row_spec(d), row_spec(mix.shape[1]), row_spec(MEM_WIDTH),
            pl.BlockSpec((1,) + mem_kv.shape[1:], lambda i: (i // tiles_per_seq, 0, 0)),
            _const_spec(w_out.shape), _const_spec((1, d)),
            _const_spec(w_gate.shape), _const_spec(w_up.shape), _const_spec(w_down.shape),
            _const_spec((1, d)),
        ],
        out_specs=row_spec(d),
        out_shape=jax.ShapeDtypeStruct((rows, d), F32),
        scratch_shapes=[pltpu.VMEM((tm, D_FF), BF16)],
        compiler_params=_params(1),
        name="post_final" if final else "post",
    )(x, mix, mem_q, mem_kv, w_out, g_ffn, w_gate, w_up, w_down, g_last)


def _rot_columns(w, heads):
    d = w.shape[0]
    wh = w.reshape(d, heads, 2, HEAD_DIM // 2)
    return jnp.concatenate([-wh[:, :, 1], wh[:, :, 0]], axis=2).reshape(d, heads * HEAD_DIM)


def _pack_nsa_weight(w):
    sizes = [NSA_WIDTH] + [NSA_KV_WIDTH] * 6 + [N_BRANCH * NSA_HEADS, MEM_WIDTH]
    parts = []
    off = 0
    for sz in sizes:
        parts.append(w[:, off:off + sz])
        off += sz
    q, kc, vc, ks, vs, kw, vw, g, mq = parts
    g = jnp.pad(g, ((0, 0), (0, LANES - g.shape[1])))
    packed = [q, _rot_columns(q, NSA_HEADS)]
    for k in (kc, ks, kw):
        packed += [k, _rot_columns(k, NSA_KV_HEADS)]
    packed += [vc, vs, vw, mq, g]
    return jnp.concatenate(packed, axis=1).astype(BF16)


def _rope_tables(seq):
    half = HEAD_DIM // 2
    pos = jnp.arange(seq, dtype=F32)
    inv = ROPE_THETA ** (-jnp.arange(half, dtype=F32) / half)
    ang = pos[:, None] * inv[None, :]
    reps = LANES // half
    return jnp.tile(jnp.cos(ang), (1, reps)), jnp.tile(jnp.sin(ang), (1, reps))


def _selection_map_t(seq):
    n_rows = seq // CMP_STRIDE
    n_blk = seq // SEL_BLOCK
    c0 = jnp.arange(n_rows) * CMP_STRIDE
    b0 = jnp.arange(n_blk) * SEL_BLOCK
    lo = jnp.maximum(c0[None, :], b0[:, None])
    hi = jnp.minimum(c0[None, :] + CMP_LEN, b0[:, None] + SEL_BLOCK)
    frac = jnp.clip(hi - lo, 0).astype(F32) / CMP_LEN
    return jnp.where(jnp.arange(n_rows)[None, :] < n_rows - 1, frac, 0.0).astype(BF16)


def _heads_major(t, batch, seq, heads):
    return t.reshape(batch, seq, heads, HEAD_DIM).transpose(0, 2, 1, 3)


def _nsa_layer_mixer(x, gain, w_in, pe_k, pe_v, ck_w1, ck_w2, cv_w1, cv_w2, batch, seq):
    cos, sin = _rope_tables(seq)
    q, kc, vc, ks, vs, kw, vw, mem_q, gate = _nsa_in_proj(x, gain, _pack_nsa_weight(w_in), cos, sin, seq)

    n_rows = seq // CMP_STRIDE
    half = CMP_STRIDE * HEAD_DIM
    chunked = lambda t: _heads_major(t, batch, seq, NSA_KV_HEADS).reshape(batch, NSA_KV_HEADS, n_rows, half)
    pe2 = lambda pe: pe.reshape(2, half)
    kcmp, vcmp = _compress(
        chunked(kc), chunked(vc), pe2(pe_k), pe2(pe_v),
        ck_w1[:half].astype(BF16), ck_w1[half:].astype(BF16), ck_w2.astype(BF16),
        cv_w1[:half].astype(BF16), cv_w1[half:].astype(BF16), cv_w2.astype(BF16))

    hm = lambda t: _heads_major(t, batch, seq, NSA_KV_HEADS)
    front = lambda t: jnp.pad(hm(t), ((0, 0), (0, 0), (WINDOW, 0), (0, 0)))
    n_q = seq // Q_BLOCK
    gates = gate[:, :N_BRANCH * NSA_HEADS].reshape(batch, n_q, Q_BLOCK, NSA_KV_HEADS, NSA_GROUP, N_BRANCH)
    gates = gates.transpose(0, 3, 1, 5, 4, 2).reshape(batch, NSA_KV_HEADS, n_q, N_BRANCH, Q_COLS)
    n_blk = seq // SEL_BLOCK
    ltri = (jnp.arange(n_blk)[None, :] < jnp.arange(n_blk)[:, None]).astype(BF16)
    o = _nsa_attention(_heads_major(q, batch, seq, NSA_HEADS), kcmp, vcmp, hm(ks), hm(vs),
                       front(kw), front(vw), gates, _selection_map_t(seq), ltri)
    mix = o.transpose(0, 2, 1, 3).reshape(batch * seq, NSA_WIDTH)
    return mix, mem_q


def kernel(x, mem, norm_mix, norm_ffn, norm_mem, norm_final, w_mem_kv, ffn_w_gate, ffn_w_up, ffn_w_down,
           gmlp_w_in, gmlp_v_norm, gmlp_w_s, gmlp_b_s, gmlp_w_out,
           nsa_w_in, nsa_pe_k, nsa_pe_v, nsa_ck_w1, nsa_ck_w2, nsa_cv_w1, nsa_cv_w2, nsa_w_out):
    batch, seq, d = x.shape
    mem_len = mem.shape[1]
    depth = norm_mix.shape[0]
    assert seq % max(ROW_TILE, KEY_TILE) == 0 and d == D_MODEL
    row = lambda v: v.reshape(1, -1)
    xt = x.reshape(batch * seq, d)
    mem_t = mem.reshape(batch * mem_len, d)
    for i in range(depth):
        j = i // 2
        mem_kv = _norm_matmul(mem_t, row(norm_mem[i]), w_mem_kv[i].astype(BF16), BF16)
        mem_kv = mem_kv.reshape(batch, mem_len, 2 * MEM_WIDTH)
        if i % 2 == 0:
            b_full = jnp.repeat(gmlp_b_s[j].T, GMLP_WIDTH // GMLP_GROUPS, axis=1)
            mix, mem_q = _gmlp_mixer(xt, row(norm_mix[i]), gmlp_w_in[j].astype(BF16), row(gmlp_v_norm[j]),
                                     gmlp_w_s[j].astype(BF16), b_full)
            w_out = gmlp_w_out[j]
        else:
            mix, mem_q = _nsa_layer_mixer(xt, row(norm_mix[i]), nsa_w_in[j], nsa_pe_k[j], nsa_pe_v[j],
                                          nsa_ck_w1[j], nsa_ck_w2[j], nsa_cv_w1[j], nsa_cv_w2[j], batch, seq)
            w_out = nsa_w_out[j]
        xt = _post(xt, mix, mem_q, mem_kv, w_out.astype(BF16), row(norm_ffn[i]),
                   ffn_w_gate[i].astype(BF16), ffn_w_up[i].astype(BF16), ffn_w_down[i].astype(BF16),
                   row(norm_final), seq, final=(i == depth - 1))
    return xt.reshape(batch, seq, d)
```

```python
import functools

import jax
import jax.numpy as jnp
from jax import lax
from jax.experimental import pallas as pl
from jax.experimental.pallas import tpu as pltpu

F32 = jnp.float32
BF16 = jnp.bfloat16

D_MODEL = 1024
HEAD_DIM = 64
NSA_HEADS = 12
NSA_KV_HEADS = 4
NSA_GROUP = NSA_HEADS // NSA_KV_HEADS
NSA_WIDTH = NSA_HEADS * HEAD_DIM
NSA_KV_WIDTH = NSA_KV_HEADS * HEAD_DIM
N_BRANCH = 3
CMP_LEN = 32
CMP_STRIDE = 16
CMP_HIDDEN = 4 * HEAD_DIM
SEL_BLOCK = 64
SEL_TOP = 16
WINDOW = 512
GMLP_CHUNK = 128
GMLP_GROUPS = 12
GMLP_WIDTH = 3 * D_MODEL // 4
MEM_HEADS = 4
MEM_WIDTH = MEM_HEADS * HEAD_DIM
D_FF = 2816
ROPE_THETA = 10000.0
NORM_EPS = 1e-6
NEG_INF = -1e30
SCALE = HEAD_DIM ** -0.5

LANES = 128
ROW_TILE = 512
KEY_TILE = 512
BLK_PER_TILE = KEY_TILE // SEL_BLOCK
SEL_UNROLL = 4
FF_CHUNK = 256
Q_STEP = 256
Q_COLS = NSA_GROUP * Q_STEP
Q_PAD = 2 * LANES
AUG_ROWS = 16
ACC_ROWS = HEAD_DIM + 8
FORCED_KEY = 0x7F000000
MASK_BIAS = -(2.0 ** 100)
LOG2E = 1.4426950408889634
VMEM_LIMIT = 56 * 1024 * 1024


def _const_spec(shape):
    zeros = (0,) * len(shape)
    return pl.BlockSpec(shape, lambda *_: zeros, pipeline_mode=pl.Buffered(1))


def _params(n_axes):
    return pltpu.CompilerParams(
        dimension_semantics=("arbitrary",) * n_axes, vmem_limit_bytes=VMEM_LIMIT)


def _rms(xf, gain):
    ms = jnp.mean(xf * xf, axis=-1, keepdims=True)
    return (xf * lax.rsqrt(ms + NORM_EPS)) * gain


def _dot(a, b):
    return jnp.dot(a, b, preferred_element_type=F32)


def _dot_nt(a, b):
    return lax.dot_general(a, b, (((1,), (1,)), ((), ())), preferred_element_type=F32)


def _dot_tn(a, b):
    return lax.dot_general(a, b, (((0,), (0,)), ((), ())), preferred_element_type=F32)


def _sigmoid(x):
    return 1.0 / (1.0 + jnp.exp(-x))


def _gelu_tanh(x):
    c = 0.7978845608028654
    return 0.5 * x * (1.0 + jnp.tanh(c * (x + 0.044715 * (x * x * x))))


def _norm_matmul_kernel(x_ref, g_ref, w_ref, o_ref):
    h = _rms(x_ref[...], g_ref[...]).astype(BF16)
    o_ref[...] = _dot(h, w_ref[...]).astype(o_ref.dtype)


def _norm_matmul(x, gain, w, out_dtype):
    rows, d = x.shape
    n = w.shape[1]
    tm = min(ROW_TILE, rows)
    return pl.pallas_call(
        _norm_matmul_kernel,
        grid=(rows // tm,),
        in_specs=[pl.BlockSpec((tm, d), lambda i: (i, 0)), _const_spec((1, d)), _const_spec((d, n))],
        out_specs=pl.BlockSpec((tm, n), lambda i: (i, 0)),
        out_shape=jax.ShapeDtypeStruct((rows, n), out_dtype),
        compiler_params=_params(1),
        name="mem_kv",
    )(x, gain, w)


def _gmlp_kernel(x_ref, g_ref, win_ref, vg_ref, ws_ref, bs_ref, mix_ref, mq_ref):
    tm = x_ref.shape[0]
    h = _rms(x_ref[...], g_ref[...]).astype(BF16)
    mq_ref[...] = _dot(h, win_ref[:, 2 * GMLP_WIDTH:]).astype(BF16)
    u = _gelu_tanh(_dot(h, win_ref[:, :GMLP_WIDTH]))
    v = _gelu_tanh(_dot(h, win_ref[:, GMLP_WIDTH:2 * GMLP_WIDTH]))
    vn = _rms(v, vg_ref[...]).astype(BF16)

    row = lax.broadcasted_iota(jnp.int32, (GMLP_CHUNK, GMLP_CHUNK), 0)
    col = lax.broadcasted_iota(jnp.int32, (GMLP_CHUNK, GMLP_CHUNK), 1)
    causal = col <= row
    first_group = lax.broadcasted_iota(jnp.int32, (GMLP_CHUNK, LANES), 1) < (LANES // 2)
    zero = jnp.zeros((), BF16)
    for slab in range(GMLP_WIDTH // LANES):
        w0 = jnp.where(causal, ws_ref[2 * slab], zero)
        w1 = jnp.where(causal, ws_ref[2 * slab + 1], zero)
        cols = slice(slab * LANES, (slab + 1) * LANES)
        bias = bs_ref[:, cols]
        for c in range(tm // GMLP_CHUNK):
            rows = slice(c * GMLP_CHUNK, (c + 1) * GMLP_CHUNK)
            vv = vn[rows, cols]
            mixed = jnp.where(first_group, _dot(w0, vv), _dot(w1, vv)) + bias
            mix_ref[rows, cols] = (u[rows, cols] * mixed).astype(BF16)


def _gmlp_mixer(x, gain, w_in, v_gain, w_s, b_full):
    rows, d = x.shape
    tm = ROW_TILE
    return pl.pallas_call(
        _gmlp_kernel,
        grid=(rows // tm,),
        in_specs=[
            pl.BlockSpec((tm, d), lambda i: (i, 0)),
            _const_spec((1, d)),
            _const_spec(w_in.shape),
            _const_spec((1, GMLP_WIDTH)),
            _const_spec(w_s.shape),
            _const_spec(b_full.shape),
        ],
        out_specs=[
            pl.BlockSpec((tm, GMLP_WIDTH), lambda i: (i, 0)),
            pl.BlockSpec((tm, MEM_WIDTH), lambda i: (i, 0)),
        ],
        out_shape=[
            jax.ShapeDtypeStruct((rows, GMLP_WIDTH), BF16),
            jax.ShapeDtypeStruct((rows, MEM_WIDTH), BF16),
        ],
        compiler_params=_params(1),
        name="gmlp_mixer",
    )(x, gain, w_in, v_gain, w_s, b_full)


_QW = NSA_KV_HEADS * Q_PAD
_Q0 = 0
_KC0 = _Q0 + 2 * _QW
_KS0 = _KC0 + 2 * NSA_KV_WIDTH
_KW0 = _KS0 + 2 * NSA_KV_WIDTH
_VC0 = _KW0 + 2 * NSA_KV_WIDTH
_VS0 = _VC0 + NSA_KV_WIDTH
_VW0 = _VS0 + NSA_KV_WIDTH
_MQ0 = _VW0 + NSA_KV_WIDTH
_G0 = _MQ0 + MEM_WIDTH
_NSA_PACKED = _G0 + LANES


def _nsa_in_kernel(x_ref, g_ref, w_ref, cos_ref, sin_ref,
                   q_ref, kc_ref, vc_ref, ks_ref, vs_ref, kw_ref, vw_ref, mq_ref, gate_ref, *, tiles_per_seq):
    tm = x_ref.shape[0]
    h = _rms(x_ref[...], g_ref[...]).astype(BF16)
    cos = cos_ref[...]
    sin = sin_ref[...]

    def proj(c0, width):
        return _dot(h, w_ref[:, c0:c0 + width])

    def roped(c0, width):
        reps = width // LANES
        cw = jnp.concatenate([cos] * reps, axis=1)
        sw = jnp.concatenate([sin] * reps, axis=1)
        return proj(c0, width) * cw + proj(c0 + width, width) * sw

    lane = lax.broadcasted_iota(jnp.int32, (tm, LANES), 1)
    is_head = lane < HEAD_DIM
    pos = lax.broadcasted_iota(jnp.int32, (tm, LANES), 0) + (pl.program_id(0) % tiles_per_seq) * tm
    blk_in_tile = lax.shift_right_logical(pos, SEL_BLOCK.bit_length() - 1) & (BLK_PER_TILE - 1)
    blk_flag = jnp.where((lane - HEAD_DIM) == blk_in_tile, 1.0, 0.0)
    one_flag = jnp.where(lane == HEAD_DIM, 1.0, 0.0)
    no_flag = jnp.zeros((tm, LANES), F32)

    def head_slabs(t, extra, o_ref):
        for pair in range(NSA_KV_HEADS // 2):
            both = t[:, pair * LANES:(pair + 1) * LANES]
            swapped = pltpu.roll(both, HEAD_DIM, 1)
            for g, src in ((2 * pair, both), (2 * pair + 1, swapped)):
                slab = jnp.where(is_head, src, extra)
                o_ref[0, g] = slab[:, :o_ref.shape[3]].astype(o_ref.dtype)

    q_ref[...] = (roped(_Q0, _QW) * (SCALE * LOG2E)).astype(BF16)
    head_slabs(roped(_KC0, NSA_KV_WIDTH), no_flag, kc_ref)
    head_slabs(roped(_KS0, NSA_KV_WIDTH), blk_flag, ks_ref)
    head_slabs(roped(_KW0, NSA_KV_WIDTH), no_flag, kw_ref)
    head_slabs(proj(_VC0, NSA_KV_WIDTH), no_flag, vc_ref)
    head_slabs(proj(_VS0, NSA_KV_WIDTH), one_flag, vs_ref)
    head_slabs(proj(_VW0, NSA_KV_WIDTH), one_flag, vw_ref)
    mq_ref[...] = proj(_MQ0, MEM_WIDTH).astype(BF16)
    gate_ref[...] = _sigmoid(proj(_G0, LANES))


def _nsa_in_proj(x, gain, w_packed, cos, sin, batch, seq):
    rows, d = x.shape
    tm = ROW_TILE
    tiles_per_seq = seq // tm
    row_spec = lambda width: pl.BlockSpec((tm, width), lambda i: (i, 0))
    tab_spec = pl.BlockSpec((tm, LANES), lambda i: (i % tiles_per_seq, 0))
    head_spec = lambda width: pl.BlockSpec(
        (1, NSA_KV_HEADS, tm, width), lambda i: (i // tiles_per_seq, 0, i % tiles_per_seq, 0))
    head_shape = lambda width, dt: jax.ShapeDtypeStruct((batch, NSA_KV_HEADS, seq, width), dt)
    return pl.pallas_call(
        functools.partial(_nsa_in_kernel, tiles_per_seq=tiles_per_seq),
        grid=(rows // tm,),
        in_specs=[row_spec(d), _const_spec((1, d)), _const_spec(w_packed.shape), tab_spec, tab_spec],
        out_specs=[row_spec(_QW), head_spec(HEAD_DIM), head_spec(HEAD_DIM),
                   head_spec(LANES), head_spec(LANES), head_spec(LANES), head_spec(LANES),
                   row_spec(MEM_WIDTH), row_spec(LANES)],
        out_shape=[jax.ShapeDtypeStruct((rows, _QW), BF16),
                   head_shape(HEAD_DIM, F32), head_shape(HEAD_DIM, F32),
                   head_shape(LANES, BF16), head_shape(LANES, BF16),
                   head_shape(LANES, BF16), head_shape(LANES, BF16),
                   jax.ShapeDtypeStruct((rows, MEM_WIDTH), BF16),
                   jax.ShapeDtypeStruct((rows, LANES), F32)],
        compiler_params=_params(1),
        name="nsa_in_proj",
    )(x, gain, w_packed, cos, sin)


def _compress_kernel(xk_ref, xv_ref, pek_ref, pev_ref, k1a_ref, k1b_ref, k2_ref,
                     v1a_ref, v1b_ref, v2_ref, ok_ref, ov_ref):
    n = xk_ref.shape[2]
    keep = lax.broadcasted_iota(jnp.int32, (n, HEAD_DIM), 0) < (n - 1)

    def one(x_ref, pe_ref, w1a_ref, w1b_ref, w2_ref, o_ref):
        x = x_ref[0, 0]
        first = _dot((x + pe_ref[0:1, :]).astype(BF16), w1a_ref[...])
        second = _dot((x + pe_ref[1:2, :]).astype(BF16), w1b_ref[...])
        pre = first + pltpu.roll(second, n - 1, 0)
        hidden = (pre * _sigmoid(pre)).astype(BF16)
        out = _dot(hidden, w2_ref[...])
        o_ref[0, 0] = jnp.where(keep, out, 0.0).astype(BF16)

    one(xk_ref, pek_ref, k1a_ref, k1b_ref, k2_ref, ok_ref)
    one(xv_ref, pev_ref, v1a_ref, v1b_ref, v2_ref, ov_ref)


def _compress(xk, xv, pek, pev, k1a, k1b, k2, v1a, v1b, v2):
    b, g, n, w = xk.shape
    x_spec = pl.BlockSpec((1, 1, n, w), lambda i, j: (i, j, 0, 0))
    o_spec = pl.BlockSpec((1, 1, n, HEAD_DIM), lambda i, j: (i, j, 0, 0))
    consts = [pek, pev, k1a, k1b, k2, v1a, v1b, v2]
    return pl.pallas_call(
        _compress_kernel,
        grid=(b, g),
        in_specs=[x_spec, x_spec] + [_const_spec(c.shape) for c in consts],
        out_specs=[o_spec, o_spec],
        out_shape=[jax.ShapeDtypeStruct((b, g, n, HEAD_DIM), BF16)] * 2,
        compiler_params=_params(2),
        name="nsa_compress",
    )(xk, xv, *consts)


def _tile3(a):
    return jnp.concatenate([a] * NSA_GROUP, axis=1)


def _nsa_attn_kernel(q_ref, kc_ref, vc_ref, ks_ref, vs_ref, kw_ref, vw_ref, gate_ref,
                     selt_ref, ltri_ref, band_ref, o_ref, bias_ref):
    i = pl.program_id(2)
    n_cmp = kc_ref.shape[2]
    n_blk = selt_ref.shape[0]
    t0 = i * Q_STEP
    q_dt = q_ref[...].astype(F32).T
    q_t = jnp.concatenate([q_dt[r * HEAD_DIM:(r + 1) * HEAD_DIM] for r in range(NSA_GROUP)],
                          axis=1).astype(BF16)

    c_idx = lax.broadcasted_iota(jnp.int32, (n_cmp, Q_STEP), 0)
    t_idx = lax.broadcasted_iota(jnp.int32, (n_cmp, Q_STEP), 1) + t0
    ok_c = _tile3(c_idx * CMP_STRIDE + (CMP_LEN - 1) <= t_idx)
    s_c = jnp.where(ok_c, _dot(kc_ref[0, 0], q_t), NEG_INF)
    e_c = jnp.exp2(s_c - jnp.max(s_c, axis=0, keepdims=True))
    inv_c = 1.0 / jnp.sum(e_c, axis=0, keepdims=True)
    t_row = lax.broadcasted_iota(jnp.int32, (1, Q_STEP), 1) + t0
    any_c = _tile3((t_row >= CMP_LEN - 1).astype(F32))
    p_c = e_c * (inv_c * any_c)
    o_cmp = _dot_tn(vc_ref[0, 0], p_c.astype(BF16))

    p_sum = p_c[:, :Q_STEP] + p_c[:, Q_STEP:2 * Q_STEP] + p_c[:, 2 * Q_STEP:]
    p_hi = p_sum.astype(BF16)
    p_lo = (p_sum - p_hi.astype(F32)).astype(BF16)
    imp = _dot(selt_ref[...], p_hi) + _dot(selt_ref[...], p_lo)

    blk = lax.broadcasted_iota(jnp.int32, (n_blk, Q_STEP), 0)
    tok = lax.broadcasted_iota(jnp.int32, (n_blk, Q_STEP), 1)
    cur = i * (Q_STEP // SEL_BLOCK) + lax.shift_right_logical(tok, SEL_BLOCK.bit_length() - 1)
    forced = (blk == 0) | (blk == cur) | (blk == cur - 1)
    key = lax.bitcast_convert_type(jnp.maximum(imp, 0.0), jnp.int32)
    key = jnp.where(forced, FORCED_KEY, key)
    key = jnp.where(blk <= cur, key, -1)
    top = float(min(SEL_TOP, n_blk))

    pad_rows = jnp.zeros((ks_ref.shape[3] - HEAD_DIM - AUG_ROWS, Q_COLS), BF16)
    span = WINDOW + Q_STEP
    back = jnp.minimum(i, WINDOW // Q_STEP)
    w0 = pl.multiple_of(t0 - back * Q_STEP, Q_STEP)
    rhs_w = jnp.concatenate([q_t, jnp.zeros((AUG_ROWS, Q_COLS), BF16), pad_rows], axis=0)
    s_w = _dot(kw_ref[0, 0, pl.ds(w0, span), :], rhs_w) + _tile3(band_ref[back])
    e_w = jnp.exp2((s_w - jnp.max(s_w, axis=0, keepdims=True)).astype(BF16))
    acc_w = _dot_tn(vw_ref[0, 0, pl.ds(w0, span), :], e_w)[:ACC_ROWS]
    o_win = acc_w[:HEAD_DIM] * (1.0 / acc_w[HEAD_DIM:HEAD_DIM + 1])

    thr = jnp.zeros((1, Q_STEP), jnp.int32)
    for bit in range(30, -1, -1):
        cand = thr | (1 << bit)
        cnt = jnp.sum(jnp.where(key >= cand, 1.0, 0.0), axis=0, keepdims=True)
        thr = jnp.where(cnt >= top, cand, thr)
    above = key > thr
    tied = key == thr
    need = top - jnp.sum(jnp.where(above, 1.0, 0.0), axis=0, keepdims=True)
    before = _dot(ltri_ref[...], jnp.where(tied, 1.0, 0.0).astype(BF16))
    picked = above | (tied & (before < need))
    bias_ref[...] = _tile3(jnp.where(picked, 0.0, MASK_BIAS))

    def scores(j):
        k0 = pl.multiple_of(j * KEY_TILE, KEY_TILE)
        b0 = pl.multiple_of(j * BLK_PER_TILE, BLK_PER_TILE)
        rows = bias_ref[pl.ds(b0, BLK_PER_TILE), :]
        aug = jnp.concatenate([rows, jnp.zeros((AUG_ROWS - BLK_PER_TILE, Q_COLS), F32)], axis=0).astype(BF16)
        rhs = jnp.concatenate([q_t, aug, pad_rows], axis=0)
        return _dot(ks_ref[0, 0, pl.ds(k0, KEY_TILE), :], rhs)

    def sel_tiles(tiles, carry, causal_last):
        m_run, acc = carry
        all_s = [scores(j) for j in tiles]
        for n, (j, s) in enumerate(zip(tiles, all_s)):
            if causal_last and n == len(tiles) - 1:
                pos = lax.broadcasted_iota(jnp.int32, (KEY_TILE, Q_STEP), 0) + j * KEY_TILE
                tq = lax.broadcasted_iota(jnp.int32, (KEY_TILE, Q_STEP), 1) + t0
                s = jnp.where(_tile3(pos <= tq), s, NEG_INF)
            m_new = jnp.maximum(m_run, jnp.max(s, axis=0, keepdims=True))
            alpha = jnp.exp2(m_run - m_new)
            p = jnp.exp2((s - m_new).astype(BF16))
            k0 = pl.multiple_of(j * KEY_TILE, KEY_TILE)
            pv = _dot_tn(vs_ref[0, 0, pl.ds(k0, KEY_TILE), :], p)[:ACC_ROWS]
            m_run, acc = m_new, alpha * acc + pv
        return m_run, acc

    init = (jnp.full((1, Q_COLS), NEG_INF, F32), jnp.zeros((ACC_ROWS, Q_COLS), F32))
    last = t0 // KEY_TILE
    carry = lax.fori_loop(
        0, last // SEL_UNROLL,
        lambda j, c: sel_tiles([SEL_UNROLL * j + n for n in range(SEL_UNROLL)], c, False), init)
    rem = last % SEL_UNROLL
    _, acc_sel = lax.switch(
        rem, [functools.partial(lambda c, n: sel_tiles([last - n + d for d in range(n + 1)], c, True), n=n)
              for n in range(SEL_UNROLL)], carry)
    o_sel = acc_sel[:HEAD_DIM] * (1.0 / acc_sel[HEAD_DIM:HEAD_DIM + 1])

    gates = gate_ref[0, 0, 0]
    o_t = gates[0:1, :] * o_cmp + gates[1:2, :] * o_sel + gates[2:3, :] * o_win
    o_dt = jnp.concatenate([o_t[:, r * Q_STEP:(r + 1) * Q_STEP] for r in range(NSA_GROUP)]
                           + [jnp.zeros((Q_PAD - NSA_GROUP * HEAD_DIM, Q_STEP), F32)], axis=0)
    o_ref[...] = o_dt.T.astype(BF16)


def _nsa_attention(q, kc, vc, ks, vs, kw, vw, gates, sel_t, ltri, band):
    rows = q.shape[0]
    b, _, seq, _ = ks.shape
    n_cmp = kc.shape[2]
    n_blk = seq // SEL_BLOCK
    n_q = seq // Q_STEP
    per_group = lambda n, width: pl.BlockSpec((1, 1, n, width), lambda bi, gi, qi: (bi, gi, 0, 0))
    q_spec = pl.BlockSpec((Q_STEP, Q_PAD), lambda bi, gi, qi: (bi * n_q + qi, gi))
    return pl.pallas_call(
        _nsa_attn_kernel,
        grid=(b, NSA_KV_HEADS, n_q),
        in_specs=[
            q_spec,
            per_group(n_cmp, HEAD_DIM), per_group(n_cmp, HEAD_DIM),
            per_group(seq, LANES), per_group(seq, LANES),
            per_group(seq, LANES), per_group(seq, LANES),
            pl.BlockSpec((1, 1, 1, N_BRANCH, Q_COLS), lambda bi, gi, qi: (bi, gi, qi, 0, 0)),
            _const_spec(sel_t.shape),
            _const_spec(ltri.shape),
            _const_spec(band.shape),
        ],
        out_specs=q_spec,
        out_shape=jax.ShapeDtypeStruct((rows, _QW), BF16),
        scratch_shapes=[pltpu.VMEM((n_blk, Q_COLS), F32)],
        compiler_params=_params(3),
        name="nsa_attention",
    )(q, kc, vc, ks, vs, kw, vw, gates, sel_t, ltri, band)


def _post_kernel(x_ref, mix_ref, mq_ref, mkv_ref, wo_ref, gf_ref, wg_ref, wu_ref, wd_ref, gl_ref,
                 o_ref, act_ref, *, final):
    tm = x_ref.shape[0]
    mq = mq_ref[...]
    mk = mkv_ref[0, :, :MEM_WIDTH]
    mv = mkv_ref[0, :, MEM_WIDTH:]
    lane = lax.broadcasted_iota(jnp.int32, (tm, MEM_WIDTH), 1)
    mem_o = jnp.zeros((tm, MEM_WIDTH), F32)
    zero = jnp.zeros((), BF16)
    for hd in range(MEM_HEADS):
        in_head = (lane >= hd * HEAD_DIM) & (lane < (hd + 1) * HEAD_DIM)
        s = _dot_nt(jnp.where(in_head, mq, zero), mk) * SCALE
        m = jnp.max(s, axis=-1, keepdims=True)
        e = jnp.exp(s - m)
        p = e * (1.0 / jnp.sum(e, axis=-1, keepdims=True))
        mem_o = jnp.where(in_head, _dot(p.astype(BF16), mv), mem_o)
    mix_w = mix_ref.shape[1]
    y = _dot(mix_ref[...], wo_ref[:mix_w, :]) + _dot(mem_o.astype(BF16), wo_ref[mix_w:, :])
    x1 = x_ref[...] + y

    h = _rms(x1, gf_ref[...]).astype(BF16)
    for c in range(D_FF // FF_CHUNK):
        cols = slice(c * FF_CHUNK, (c + 1) * FF_CHUNK)
        gate = _dot(h, wg_ref[:, cols])
        up = _dot(h, wu_ref[:, cols])
        act_ref[:, cols] = (gate * _sigmoid(gate) * up).astype(BF16)
    x2 = x1 + _dot(act_ref[...], wd_ref[...])
    if final:
        x2 = _rms(x2, gl_ref[...])
    o_ref[...] = x2


def _post(x, mix, mem_q, mem_kv, w_out, g_ffn, w_gate, w_up, w_down, g_last, seq, final):
    rows, d = x.shape
    tm = ROW_TILE
    tiles_per_seq = seq // tm
    row_spec = lambda width: pl.BlockSpec((tm, width), lambda i: (i, 0))
    return pl.pallas_call(
        functools.partial(_post_kernel, final=final),
        grid=(rows // tm,),
        in_specs=[
            row_spec(d), row_spec(mix.shape[1]), row_spec(MEM_WIDTH),
            pl.BlockSpec((1,) + mem_kv.shape[1:], lambda i: (i // tiles_per_seq, 0, 0)),
            _const_spec(w_out.shape), _const_spec((1, d)),
            _const_spec(w_gate.shape), _const_spec(w_up.shape), _const_spec(w_down.shape),
            _const_spec((1, d)),
        ],
        out_specs=row_spec(d),
        out_shape=jax.ShapeDtypeStruct((rows, d), F32),
        scratch_shapes=[pltpu.VMEM((tm, D_FF), BF16)],
        compiler_params=_params(1),
        name="post_final" if final else "post",
    )(x, mix, mem_q, mem_kv, w_out, g_ffn, w_gate, w_up, w_down, g_last)


def _rot_columns(w, heads):
    d = w.shape[0]
    wh = w.reshape(d, heads, 2, HEAD_DIM // 2)
    return jnp.concatenate([-wh[:, :, 1], wh[:, :, 0]], axis=2).reshape(d, heads * HEAD_DIM)


def _pad_groups(w, axis):
    shape = list(w.shape)
    shape[axis:axis + 1] = [NSA_KV_HEADS, NSA_GROUP * HEAD_DIM]
    pad = [(0, 0)] * len(shape)
    pad[axis + 1] = (0, Q_PAD - NSA_GROUP * HEAD_DIM)
    out_shape = list(w.shape)
    out_shape[axis] = _QW
    return jnp.pad(w.reshape(shape), pad).reshape(out_shape)


def _pack_nsa_weight(w):
    sizes = [NSA_WIDTH] + [NSA_KV_WIDTH] * 6 + [N_BRANCH * NSA_HEADS, MEM_WIDTH]
    parts = []
    off = 0
    for sz in sizes:
        parts.append(w[:, off:off + sz])
        off += sz
    q, kc, vc, ks, vs, kw, vw, g, mq = parts
    g = jnp.pad(g, ((0, 0), (0, LANES - g.shape[1])))
    packed = [_pad_groups(q, 1), _pad_groups(_rot_columns(q, NSA_HEADS), 1)]
    for k in (kc, ks, kw):
        packed += [k, _rot_columns(k, NSA_KV_HEADS)]
    packed += [vc, vs, vw, mq, g]
    return jnp.concatenate(packed, axis=1).astype(BF16)


def _rope_tables(seq):
    half = HEAD_DIM // 2
    pos = jnp.arange(seq, dtype=F32)
    inv = ROPE_THETA ** (-jnp.arange(half, dtype=F32) / half)
    ang = pos[:, None] * inv[None, :]
    reps = LANES // half
    return jnp.tile(jnp.cos(ang), (1, reps)), jnp.tile(jnp.sin(ang), (1, reps))


def _selection_map_t(seq):
    n_rows = seq // CMP_STRIDE
    n_blk = seq // SEL_BLOCK
    c0 = jnp.arange(n_rows) * CMP_STRIDE
    b0 = jnp.arange(n_blk) * SEL_BLOCK
    lo = jnp.maximum(c0[None, :], b0[:, None])
    hi = jnp.minimum(c0[None, :] + CMP_LEN, b0[:, None] + SEL_BLOCK)
    frac = jnp.clip(hi - lo, 0).astype(F32) / CMP_LEN
    return jnp.where(jnp.arange(n_rows)[None, :] < n_rows - 1, frac, 0.0).astype(BF16)


def _window_bands():
    back = jnp.arange(WINDOW // Q_STEP + 1)[:, None, None]
    k_loc = jnp.arange(WINDOW + Q_STEP)[None, :, None]
    t_loc = jnp.arange(Q_STEP)[None, None, :]
    rel = k_loc - back * Q_STEP - t_loc
    return jnp.where((rel <= 0) & (rel > -WINDOW), 0.0, NEG_INF).astype(F32)


def _nsa_layer_mixer(x, gain, w_in, pe_k, pe_v, ck_w1, ck_w2, cv_w1, cv_w2, batch, seq):
    cos, sin = _rope_tables(seq)
    q, kc, vc, ks, vs, kw, vw, mem_q, gate = _nsa_in_proj(
        x, gain, _pack_nsa_weight(w_in), cos, sin, batch, seq)

    n_rows = seq // CMP_STRIDE
    half = CMP_STRIDE * HEAD_DIM
    chunked = lambda t: t.reshape(batch, NSA_KV_HEADS, n_rows, half)
    pe2 = lambda pe: pe.reshape(2, half)
    kcmp, vcmp = _compress(
        chunked(kc), chunked(vc), pe2(pe_k), pe2(pe_v),
        ck_w1[:half].astype(BF16), ck_w1[half:].astype(BF16), ck_w2.astype(BF16),
        cv_w1[:half].astype(BF16), cv_w1[half:].astype(BF16), cv_w2.astype(BF16))

    n_q = seq // Q_STEP
    gates = gate[:, :N_BRANCH * NSA_HEADS].reshape(batch, n_q, Q_STEP, NSA_KV_HEADS, NSA_GROUP, N_BRANCH)
    gates = gates.transpose(0, 3, 1, 5, 4, 2).reshape(batch, NSA_KV_HEADS, n_q, N_BRANCH, Q_COLS)
    n_blk = seq // SEL_BLOCK
    ltri = (jnp.arange(n_blk)[None, :] < jnp.arange(n_blk)[:, None]).astype(BF16)
    mix = _nsa_attention(q, kcmp, vcmp, ks, vs, kw, vw, gates, _selection_map_t(seq), ltri, _window_bands())
    return mix, mem_q


def kernel(x, mem, norm_mix, norm_ffn, norm_mem, norm_final, w_mem_kv, ffn_w_gate, ffn_w_up, ffn_w_down,
           gmlp_w_in, gmlp_v_norm, gmlp_w_s, gmlp_b_s, gmlp_w_out,
           nsa_w_in, nsa_pe_k, nsa_pe_v, nsa_ck_w1, nsa_ck_w2, nsa_cv_w1, nsa_cv_w2, nsa_w_out):
    batch, seq, d = x.shape
    mem_len = mem.shape[1]
    depth = norm_mix.shape[0]
    assert seq % max(ROW_TILE, KEY_TILE) == 0 and seq >= WINDOW + Q_STEP and d == D_MODEL
    row = lambda v: v.reshape(1, -1)
    xt = x.reshape(batch * seq, d)
    mem_t = mem.reshape(batch * mem_len, d)
    for i in range(depth):
        j = i // 2
        mem_kv = _norm_matmul(mem_t, row(norm_mem[i]), w_mem_kv[i].astype(BF16), BF16)
        mem_kv = mem_kv.reshape(batch, mem_len, 2 * MEM_WIDTH)
        if i % 2 == 0:
            b_full = jnp.repeat(gmlp_b_s[j].T, GMLP_WIDTH // GMLP_GROUPS, axis=1)
            mix, mem_q = _gmlp_mixer(xt, row(norm_mix[i]), gmlp_w_in[j].astype(BF16), row(gmlp_v_norm[j]),
                                     gmlp_w_s[j].astype(BF16), b_full)
            w_out = gmlp_w_out[j]
        else:
            mix, mem_q = _nsa_layer_mixer(xt, row(norm_mix[i]), nsa_w_in[j], nsa_pe_k[j], nsa_pe_v[j],
                                          nsa_ck_w1[j], nsa_ck_w2[j], nsa_cv_w1[j], nsa_cv_w2[j], batch, seq)
            w_out = jnp.concatenate([_pad_groups(nsa_w_out[j][:NSA_WIDTH], 0), nsa_w_out[j][NSA_WIDTH:]], axis=0)
        xt = _post(xt, mix, mem_q, mem_kv, w_out.astype(BF16), row(norm_ffn[i]),
                   ffn_w_gate[i].astype(BF16), ffn_w_up[i].astype(BF16), ffn_w_down[i].astype(BF16),
                   row(norm_final), seq, final=(i == depth - 1))
    return xt.reshape(batch, seq, d)
```

```python
import functools

import jax
import jax.numpy as jnp
from jax import lax
from jax.experimental import pallas as pl
from jax.experimental.pallas import tpu as pltpu

F32 = jnp.float32
BF16 = jnp.bfloat16

D_MODEL = 1024
HEAD_DIM = 64
NSA_HEADS = 12
NSA_KV_HEADS = 4
NSA_GROUP = NSA_HEADS // NSA_KV_HEADS
NSA_WIDTH = NSA_HEADS * HEAD_DIM
NSA_KV_WIDTH = NSA_KV_HEADS * HEAD_DIM
N_BRANCH = 3
CMP_LEN = 32
CMP_STRIDE = 16
CMP_HIDDEN = 4 * HEAD_DIM
SEL_BLOCK = 64
SEL_TOP = 16
WINDOW = 512
GMLP_CHUNK = 128
GMLP_GROUPS = 12
GMLP_WIDTH = 3 * D_MODEL // 4
MEM_HEADS = 4
MEM_WIDTH = MEM_HEADS * HEAD_DIM
D_FF = 2816
ROPE_THETA = 10000.0
NORM_EPS = 1e-6
NEG_INF = -1e30
SCALE = HEAD_DIM ** -0.5

LANES = 128
ROW_TILE = 512
KEY_TILE = 512
BLK_PER_TILE = KEY_TILE // SEL_BLOCK
SEL_UNROLL = 4
FF_CHUNK = 256
Q_STEP = 512
Q_COLS = NSA_GROUP * Q_STEP
Q_PAD = 2 * LANES
AUG_ROWS = 16
ACC_ROWS = HEAD_DIM + 8
FORCED_KEY = 0x7F000000
MASK_BIAS = -(2.0 ** 100)
LOG2E = 1.4426950408889634
VMEM_LIMIT = 56 * 1024 * 1024


def _const_spec(shape):
    zeros = (0,) * len(shape)
    return pl.BlockSpec(shape, lambda *_: zeros, pipeline_mode=pl.Buffered(1))


def _params(n_axes):
    return pltpu.CompilerParams(
        dimension_semantics=("arbitrary",) * n_axes, vmem_limit_bytes=VMEM_LIMIT)


def _rms(xf, gain):
    ms = jnp.mean(xf * xf, axis=-1, keepdims=True)
    return (xf * lax.rsqrt(ms + NORM_EPS)) * gain


def _dot(a, b):
    return jnp.dot(a, b, preferred_element_type=F32)


def _dot_nt(a, b):
    return lax.dot_general(a, b, (((1,), (1,)), ((), ())), preferred_element_type=F32)


def _dot_tn(a, b):
    return lax.dot_general(a, b, (((0,), (0,)), ((), ())), preferred_element_type=F32)


def _sigmoid(x):
    return 1.0 / (1.0 + jnp.exp(-x))


def _gelu_tanh(x):
    c = 0.7978845608028654
    return 0.5 * x * (1.0 + jnp.tanh(c * (x + 0.044715 * (x * x * x))))


def _norm_matmul_kernel(x_ref, g_ref, w_ref, o_ref):
    h = _rms(x_ref[...], g_ref[...]).astype(BF16)
    o_ref[...] = _dot(h, w_ref[...]).astype(o_ref.dtype)


def _norm_matmul(x, gain, w, out_dtype):
    rows, d = x.shape
    n = w.shape[1]
    tm = min(ROW_TILE, rows)
    return pl.pallas_call(
        _norm_matmul_kernel,
        grid=(rows // tm,),
        in_specs=[pl.BlockSpec((tm, d), lambda i: (i, 0)), _const_spec((1, d)), _const_spec((d, n))],
        out_specs=pl.BlockSpec((tm, n), lambda i: (i, 0)),
        out_shape=jax.ShapeDtypeStruct((rows, n), out_dtype),
        compiler_params=_params(1),
        name="mem_kv",
    )(x, gain, w)


def _gmlp_kernel(x_ref, g_ref, win_ref, vg_ref, ws_ref, bs_ref, mix_ref, mq_ref):
    tm = x_ref.shape[0]
    h = _rms(x_ref[...], g_ref[...]).astype(BF16)
    mq_ref[...] = _dot(h, win_ref[:, 2 * GMLP_WIDTH:]).astype(BF16)
    u = _gelu_tanh(_dot(h, win_ref[:, :GMLP_WIDTH]))
    v = _gelu_tanh(_dot(h, win_ref[:, GMLP_WIDTH:2 * GMLP_WIDTH]))
    vn = _rms(v, vg_ref[...]).astype(BF16)

    row = lax.broadcasted_iota(jnp.int32, (GMLP_CHUNK, GMLP_CHUNK), 0)
    col = lax.broadcasted_iota(jnp.int32, (GMLP_CHUNK, GMLP_CHUNK), 1)
    causal = col <= row
    first_group = lax.broadcasted_iota(jnp.int32, (GMLP_CHUNK, LANES), 1) < (LANES // 2)
    zero = jnp.zeros((), BF16)
    for slab in range(GMLP_WIDTH // LANES):
        w0 = jnp.where(causal, ws_ref[2 * slab], zero)
        w1 = jnp.where(causal, ws_ref[2 * slab + 1], zero)
        cols = slice(slab * LANES, (slab + 1) * LANES)
        bias = bs_ref[:, cols]
        for c in range(tm // GMLP_CHUNK):
            rows = slice(c * GMLP_CHUNK, (c + 1) * GMLP_CHUNK)
            vv = vn[rows, cols]
            mixed = jnp.where(first_group, _dot(w0, vv), _dot(w1, vv)) + bias
            mix_ref[rows, cols] = (u[rows, cols] * mixed).astype(BF16)


def _gmlp_mixer(x, gain, w_in, v_gain, w_s, b_full):
    rows, d = x.shape
    tm = ROW_TILE
    return pl.pallas_call(
        _gmlp_kernel,
        grid=(rows // tm,),
        in_specs=[
            pl.BlockSpec((tm, d), lambda i: (i, 0)),
            _const_spec((1, d)),
            _const_spec(w_in.shape),
            _const_spec((1, GMLP_WIDTH)),
            _const_spec(w_s.shape),
            _const_spec(b_full.shape),
        ],
        out_specs=[
            pl.BlockSpec((tm, GMLP_WIDTH), lambda i: (i, 0)),
            pl.BlockSpec((tm, MEM_WIDTH), lambda i: (i, 0)),
        ],
        out_shape=[
            jax.ShapeDtypeStruct((rows, GMLP_WIDTH), BF16),
            jax.ShapeDtypeStruct((rows, MEM_WIDTH), BF16),
        ],
        compiler_params=_params(1),
        name="gmlp_mixer",
    )(x, gain, w_in, v_gain, w_s, b_full)


_QW = NSA_KV_HEADS * Q_PAD
_Q0 = 0
_KC0 = _Q0 + 2 * _QW
_KS0 = _KC0 + 2 * NSA_KV_WIDTH
_KW0 = _KS0 + 2 * NSA_KV_WIDTH
_VC0 = _KW0 + 2 * NSA_KV_WIDTH
_VS0 = _VC0 + NSA_KV_WIDTH
_VW0 = _VS0 + NSA_KV_WIDTH
_MQ0 = _VW0 + NSA_KV_WIDTH
_G0 = _MQ0 + MEM_WIDTH
_NSA_PACKED = _G0 + LANES


def _nsa_in_kernel(x_ref, g_ref, w_ref, cos_ref, sin_ref,
                   q_ref, kc_ref, vc_ref, ks_ref, vs_ref, kw_ref, vw_ref, mq_ref, gate_ref, *, tiles_per_seq):
    tm = x_ref.shape[0]
    h = _rms(x_ref[...], g_ref[...]).astype(BF16)
    cos = cos_ref[...]
    sin = sin_ref[...]

    def proj(c0, width):
        return _dot(h, w_ref[:, c0:c0 + width])

    def roped(c0, width):
        reps = width // LANES
        cw = jnp.concatenate([cos] * reps, axis=1)
        sw = jnp.concatenate([sin] * reps, axis=1)
        return proj(c0, width) * cw + proj(c0 + width, width) * sw

    lane = lax.broadcasted_iota(jnp.int32, (tm, LANES), 1)
    is_head = lane < HEAD_DIM
    pos = lax.broadcasted_iota(jnp.int32, (tm, LANES), 0) + (pl.program_id(0) % tiles_per_seq) * tm
    blk_in_tile = lax.shift_right_logical(pos, SEL_BLOCK.bit_length() - 1) & (BLK_PER_TILE - 1)
    blk_flag = jnp.where((lane - HEAD_DIM) == blk_in_tile, 1.0, 0.0)
    one_flag = jnp.where(lane == HEAD_DIM, 1.0, 0.0)
    no_flag = jnp.zeros((tm, LANES), F32)

    def head_slabs(t, extra, o_ref):
        for pair in range(NSA_KV_HEADS // 2):
            both = t[:, pair * LANES:(pair + 1) * LANES]
            swapped = pltpu.roll(both, HEAD_DIM, 1)
            for g, src in ((2 * pair, both), (2 * pair + 1, swapped)):
                slab = jnp.where(is_head, src, extra)
                o_ref[0, g] = slab[:, :o_ref.shape[3]].astype(o_ref.dtype)

    q_ref[...] = (roped(_Q0, _QW) * (SCALE * LOG2E)).astype(BF16)
    head_slabs(roped(_KC0, NSA_KV_WIDTH), no_flag, kc_ref)
    head_slabs(roped(_KS0, NSA_KV_WIDTH), blk_flag, ks_ref)
    head_slabs(roped(_KW0, NSA_KV_WIDTH), no_flag, kw_ref)
    head_slabs(proj(_VC0, NSA_KV_WIDTH), no_flag, vc_ref)
    head_slabs(proj(_VS0, NSA_KV_WIDTH), one_flag, vs_ref)
    head_slabs(proj(_VW0, NSA_KV_WIDTH), one_flag, vw_ref)
    mq_ref[...] = proj(_MQ0, MEM_WIDTH).astype(BF16)
    gate_ref[...] = _sigmoid(proj(_G0, LANES))


def _nsa_in_proj(x, gain, w_packed, cos, sin, batch, seq):
    rows, d = x.shape
    tm = ROW_TILE
    tiles_per_seq = seq // tm
    row_spec = lambda width: pl.BlockSpec((tm, width), lambda i: (i, 0))
    tab_spec = pl.BlockSpec((tm, LANES), lambda i: (i % tiles_per_seq, 0))
    head_spec = lambda width: pl.BlockSpec(
        (1, NSA_KV_HEADS, tm, width), lambda i: (i // tiles_per_seq, 0, i % tiles_per_seq, 0))
    head_shape = lambda width, dt: jax.ShapeDtypeStruct((batch, NSA_KV_HEADS, seq, width), dt)
    return pl.pallas_call(
        functools.partial(_nsa_in_kernel, tiles_per_seq=tiles_per_seq),
        grid=(rows // tm,),
        in_specs=[row_spec(d), _const_spec((1, d)), _const_spec(w_packed.shape), tab_spec, tab_spec],
        out_specs=[row_spec(_QW), head_spec(HEAD_DIM), head_spec(HEAD_DIM),
                   head_spec(LANES), head_spec(LANES), head_spec(LANES), head_spec(LANES),
                   row_spec(MEM_WIDTH), row_spec(LANES)],
        out_shape=[jax.ShapeDtypeStruct((rows, _QW), BF16),
                   head_shape(HEAD_DIM, F32), head_shape(HEAD_DIM, F32),
                   head_shape(LANES, BF16), head_shape(LANES, BF16),
                   head_shape(LANES, BF16), head_shape(LANES, BF16),
                   jax.ShapeDtypeStruct((rows, MEM_WIDTH), BF16),
                   jax.ShapeDtypeStruct((rows, LANES), F32)],
        compiler_params=_params(1),
        name="nsa_in_proj",
    )(x, gain, w_packed, cos, sin)


def _compress_kernel(xk_ref, xv_ref, pek_ref, pev_ref, k1a_ref, k1b_ref, k2_ref,
                     v1a_ref, v1b_ref, v2_ref, ok_ref, ov_ref):
    n = xk_ref.shape[2]
    keep = lax.broadcasted_iota(jnp.int32, (n, HEAD_DIM), 0) < (n - 1)

    def one(x_ref, pe_ref, w1a_ref, w1b_ref, w2_ref, o_ref):
        x = x_ref[0, 0]
        first = _dot((x + pe_ref[0:1, :]).astype(BF16), w1a_ref[...])
        second = _dot((x + pe_ref[1:2, :]).astype(BF16), w1b_ref[...])
        pre = first + pltpu.roll(second, n - 1, 0)
        hidden = (pre * _sigmoid(pre)).astype(BF16)
        out = _dot(hidden, w2_ref[...])
        o_ref[0, 0] = jnp.where(keep, out, 0.0).astype(BF16)

    one(xk_ref, pek_ref, k1a_ref, k1b_ref, k2_ref, ok_ref)
    one(xv_ref, pev_ref, v1a_ref, v1b_ref, v2_ref, ov_ref)


def _compress(xk, xv, pek, pev, k1a, k1b, k2, v1a, v1b, v2):
    b, g, n, w = xk.shape
    x_spec = pl.BlockSpec((1, 1, n, w), lambda i, j: (i, j, 0, 0))
    o_spec = pl.BlockSpec((1, 1, n, HEAD_DIM), lambda i, j: (i, j, 0, 0))
    consts = [pek, pev, k1a, k1b, k2, v1a, v1b, v2]
    return pl.pallas_call(
        _compress_kernel,
        grid=(b, g),
        in_specs=[x_spec, x_spec] + [_const_spec(c.shape) for c in consts],
        out_specs=[o_spec, o_spec],
        out_shape=[jax.ShapeDtypeStruct((b, g, n, HEAD_DIM), BF16)] * 2,
        compiler_params=_params(2),
        name="nsa_compress",
    )(xk, xv, *consts)


def _tile3(a):
    return jnp.concatenate([a] * NSA_GROUP, axis=1)


def _nsa_attn_kernel(q_ref, kc_ref, vc_ref, ks_ref, vs_ref, kw_ref, vw_ref, gate_ref,
                     selt_ref, ltri_ref, band_ref, o_ref, bias_ref):
    i = pl.program_id(2)
    n_cmp = kc_ref.shape[2]
    n_blk = selt_ref.shape[0]
    t0 = i * Q_STEP
    q_dt = q_ref[...].astype(F32).T
    q_t = jnp.concatenate([q_dt[r * HEAD_DIM:(r + 1) * HEAD_DIM] for r in range(NSA_GROUP)],
                          axis=1).astype(BF16)

    def compressed(rows):
        c_idx = lax.broadcasted_iota(jnp.int32, (rows, Q_STEP), 0)
        t_idx = lax.broadcasted_iota(jnp.int32, (rows, Q_STEP), 1) + t0
        ok_c = _tile3(c_idx * CMP_STRIDE + (CMP_LEN - 1) <= t_idx)
        s_c = jnp.where(ok_c, _dot(kc_ref[0, 0, :rows, :], q_t), NEG_INF)
        e_c = jnp.exp2(s_c - jnp.max(s_c, axis=0, keepdims=True))
        inv_c = 1.0 / jnp.sum(e_c, axis=0, keepdims=True)
        t_row = lax.broadcasted_iota(jnp.int32, (1, Q_STEP), 1) + t0
        any_c = _tile3((t_row >= CMP_LEN - 1).astype(F32))
        p_c = e_c * (inv_c * any_c)
        o_c = _dot_tn(vc_ref[0, 0, :rows, :], p_c.astype(BF16))
        p_sum = p_c[:, :Q_STEP] + p_c[:, Q_STEP:2 * Q_STEP] + p_c[:, 2 * Q_STEP:]
        p_hi = p_sum.astype(BF16)
        p_lo = (p_sum - p_hi.astype(F32)).astype(BF16)
        return o_c, _dot(selt_ref[:, :rows], p_hi) + _dot(selt_ref[:, :rows], p_lo)

    o_cmp, imp = lax.cond(i < pl.num_programs(2) // 2,
                          lambda: compressed(n_cmp // 2), lambda: compressed(n_cmp))

    blk = lax.broadcasted_iota(jnp.int32, (n_blk, Q_STEP), 0)
    tok = lax.broadcasted_iota(jnp.int32, (n_blk, Q_STEP), 1)
    cur = i * (Q_STEP // SEL_BLOCK) + lax.shift_right_logical(tok, SEL_BLOCK.bit_length() - 1)
    forced = (blk == 0) | (blk == cur) | (blk == cur - 1)
    key = lax.bitcast_convert_type(jnp.maximum(imp, 0.0), jnp.int32)
    key = jnp.where(forced, FORCED_KEY, key)
    key = jnp.where(blk <= cur, key, -1)
    top = float(min(SEL_TOP, n_blk))

    pad_rows = jnp.zeros((ks_ref.shape[3] - HEAD_DIM - AUG_ROWS, Q_COLS), BF16)
    span = WINDOW + Q_STEP
    back = jnp.minimum(i, WINDOW // Q_STEP)
    w0 = pl.multiple_of(t0 - back * Q_STEP, Q_STEP)
    rhs_w = jnp.concatenate([q_t, jnp.zeros((AUG_ROWS, Q_COLS), BF16), pad_rows], axis=0)
    s_w = _dot(kw_ref[0, 0, pl.ds(w0, span), :], rhs_w) + _tile3(band_ref[back])
    e_w = jnp.exp2((s_w - jnp.max(s_w, axis=0, keepdims=True)).astype(BF16))
    acc_w = _dot_tn(vw_ref[0, 0, pl.ds(w0, span), :], e_w)[:ACC_ROWS]
    o_win = acc_w[:HEAD_DIM] * (1.0 / acc_w[HEAD_DIM:HEAD_DIM + 1])

    thr = jnp.zeros((1, Q_STEP), jnp.int32)
    for bit in range(30, -1, -1):
        cand = thr | (1 << bit)
        cnt = jnp.sum(jnp.where(key >= cand, 1.0, 0.0), axis=0, keepdims=True)
        thr = jnp.where(cnt >= top, cand, thr)
    above = key > thr
    tied = key == thr
    need = top - jnp.sum(jnp.where(above, 1.0, 0.0), axis=0, keepdims=True)
    before = _dot(ltri_ref[...], jnp.where(tied, 1.0, 0.0).astype(BF16))
    picked = above | (tied & (before < need))
    bias_ref[...] = _tile3(jnp.where(picked, 0.0, MASK_BIAS))

    def scores(j):
        k0 = pl.multiple_of(j * KEY_TILE, KEY_TILE)
        b0 = pl.multiple_of(j * BLK_PER_TILE, BLK_PER_TILE)
        rows = bias_ref[pl.ds(b0, BLK_PER_TILE), :]
        aug = jnp.concatenate([rows, jnp.zeros((AUG_ROWS - BLK_PER_TILE, Q_COLS), F32)], axis=0).astype(BF16)
        rhs = jnp.concatenate([q_t, aug, pad_rows], axis=0)
        return _dot(ks_ref[0, 0, pl.ds(k0, KEY_TILE), :], rhs)

    def sel_tiles(tiles, carry, causal_last):
        m_run, acc = carry
        all_s = [scores(j) for j in tiles]
        for n, (j, s) in enumerate(zip(tiles, all_s)):
            if causal_last and n == len(tiles) - 1:
                pos = lax.broadcasted_iota(jnp.int32, (KEY_TILE, Q_STEP), 0) + j * KEY_TILE
                tq = lax.broadcasted_iota(jnp.int32, (KEY_TILE, Q_STEP), 1) + t0
                s = jnp.where(_tile3(pos <= tq), s, NEG_INF)
            m_new = jnp.maximum(m_run, jnp.max(s, axis=0, keepdims=True))
            alpha = jnp.exp2(m_run - m_new)
            p = jnp.exp2((s - m_new).astype(BF16))
            k0 = pl.multiple_of(j * KEY_TILE, KEY_TILE)
            pv = _dot_tn(vs_ref[0, 0, pl.ds(k0, KEY_TILE), :], p)[:ACC_ROWS]
            m_run, acc = m_new, alpha * acc + pv
        return m_run, acc

    init = (jnp.full((1, Q_COLS), NEG_INF, F32), jnp.zeros((ACC_ROWS, Q_COLS), F32))
    last = t0 // KEY_TILE
    carry = lax.fori_loop(
        0, last // SEL_UNROLL,
        lambda j, c: sel_tiles([SEL_UNROLL * j + n for n in range(SEL_UNROLL)], c, False), init)
    rem = last % SEL_UNROLL
    _, acc_sel = lax.switch(
        rem, [functools.partial(lambda c, n: sel_tiles([last - n + d for d in range(n + 1)], c, True), n=n)
              for n in range(SEL_UNROLL)], carry)
    o_sel = acc_sel[:HEAD_DIM] * (1.0 / acc_sel[HEAD_DIM:HEAD_DIM + 1])

    gates = gate_ref[0, 0, 0]
    o_t = gates[0:1, :] * o_cmp + gates[1:2, :] * o_sel + gates[2:3, :] * o_win
    o_dt = jnp.concatenate([o_t[:, r * Q_STEP:(r + 1) * Q_STEP] for r in range(NSA_GROUP)]
                           + [jnp.zeros((Q_PAD - NSA_GROUP * HEAD_DIM, Q_STEP), F32)], axis=0)
    o_ref[...] = o_dt.T.astype(BF16)


def _nsa_attention(q, kc, vc, ks, vs, kw, vw, gates, sel_t, ltri, band):
    rows = q.shape[0]
    b, _, seq, _ = ks.shape
    n_cmp = kc.shape[2]
    n_blk = seq // SEL_BLOCK
    n_q = seq // Q_STEP
    per_group = lambda n, width: pl.BlockSpec((1, 1, n, width), lambda bi, gi, qi: (bi, gi, 0, 0))
    q_spec = pl.BlockSpec((Q_STEP, Q_PAD), lambda bi, gi, qi: (bi * n_q + qi, gi))
    return pl.pallas_call(
        _nsa_attn_kernel,
        grid=(b, NSA_KV_HEADS, n_q),
        in_specs=[
            q_spec,
            per_group(n_cmp, HEAD_DIM), per_group(n_cmp, HEAD_DIM),
            per_group(seq, LANES), per_group(seq, LANES),
            per_group(seq, LANES), per_group(seq, LANES),
            pl.BlockSpec((1, 1, 1, N_BRANCH, Q_COLS), lambda bi, gi, qi: (bi, gi, qi, 0, 0)),
            _const_spec(sel_t.shape),
            _const_spec(ltri.shape),
            _const_spec(band.shape),
        ],
        out_specs=q_spec,
        out_shape=jax.ShapeDtypeStruct((rows, _QW), BF16),
        scratch_shapes=[pltpu.VMEM((n_blk, Q_COLS), F32)],
        compiler_params=_params(3),
        name="nsa_attention",
    )(q, kc, vc, ks, vs, kw, vw, gates, sel_t, ltri, band)


def _post_kernel(x_ref, mix_ref, mq_ref, mkv_ref, wo_ref, gf_ref, wg_ref, wu_ref, wd_ref, gl_ref,
                 o_ref, act_ref, *, final):
    tm = x_ref.shape[0]
    mq = mq_ref[...]
    mk = mkv_ref[0, :, :MEM_WIDTH]
    mv = mkv_ref[0, :, MEM_WIDTH:]
    lane = lax.broadcasted_iota(jnp.int32, (tm, MEM_WIDTH), 1)
    mem_o = jnp.zeros((tm, MEM_WIDTH), F32)
    zero = jnp.zeros((), BF16)
    for hd in range(MEM_HEADS):
        in_head = (lane >= hd * HEAD_DIM) & (lane < (hd + 1) * HEAD_DIM)
        s = _dot_nt(jnp.where(in_head, mq, zero), mk) * SCALE
        m = jnp.max(s, axis=-1, keepdims=True)
        e = jnp.exp(s - m)
        p = e * (1.0 / jnp.sum(e, axis=-1, keepdims=True))
        mem_o = jnp.where(in_head, _dot(p.astype(BF16), mv), mem_o)
    mix_w = mix_ref.shape[1]
    y = _dot(mix_ref[...], wo_ref[:mix_w, :]) + _dot(mem_o.astype(BF16), wo_ref[mix_w:, :])
    x1 = x_ref[...] + y

    h = _rms(x1, gf_ref[...]).astype(BF16)
    for c in range(D_FF // FF_CHUNK):
        cols = slice(c * FF_CHUNK, (c + 1) * FF_CHUNK)
        gate = _dot(h, wg_ref[:, cols])
        up = _dot(h, wu_ref[:, cols])
        act_ref[:, cols] = (gate * _sigmoid(gate) * up).astype(BF16)
    x2 = x1 + _dot(act_ref[...], wd_ref[...])
    if final:
        x2 = _rms(x2, gl_ref[...])
    o_ref[...] = x2


def _post(x, mix, mem_q, mem_kv, w_out, g_ffn, w_gate, w_up, w_down, g_last, seq, final):
    rows, d = x.shape
    tm = ROW_TILE
    tiles_per_seq = seq // tm
    row_spec = lambda width: pl.BlockSpec((tm, width), lambda i: (i, 0))
    return pl.pallas_call(
        functools.partial(_post_kernel, final=final),
        grid=(rows // tm,),
        in_specs=[
            row_spec(d), row_spec(mix.shape[1]), row_spec(MEM_WIDTH),
            pl.BlockSpec((1,) + mem_kv.shape[1:], lambda i: (i // tiles_per_seq, 0, 0)),
            _const_spec(w_out.shape), _const_spec((1, d)),
            _const_spec(w_gate.shape), _const_spec(w_up.shape), _const_spec(w_down.shape),
            _const_spec((1, d)),
        ],
        out_specs=row_spec(d),
        out_shape=jax.ShapeDtypeStruct((rows, d), F32),
        scratch_shapes=[pltpu.VMEM((tm, D_FF), BF16)],
        compiler_params=_params(1),
        name="post_final" if final else "post",
    )(x, mix, mem_q, mem_kv, w_out, g_ffn, w_gate, w_up, w_down, g_last)


def _rot_columns(w, heads):
    d = w.shape[0]
    wh = w.reshape(d, heads, 2, HEAD_DIM // 2)
    return jnp.concatenate([-wh[:, :, 1], wh[:, :, 0]], axis=2).reshape(d, heads * HEAD_DIM)


def _pad_groups(w, axis):
    shape = list(w.shape)
    shape[axis:axis + 1] = [NSA_KV_HEADS, NSA_GROUP * HEAD_DIM]
    pad = [(0, 0)] * len(shape)
    pad[axis + 1] = (0, Q_PAD - NSA_GROUP * HEAD_DIM)
    out_shape = list(w.shape)
    out_shape[axis] = _QW
    return jnp.pad(w.reshape(shape), pad).reshape(out_shape)


def _pack_nsa_weight(w):
    sizes = [NSA_WIDTH] + [NSA_KV_WIDTH] * 6 + [N_BRANCH * NSA_HEADS, MEM_WIDTH]
    parts = []
    off = 0
    for sz in sizes:
        parts.append(w[:, off:off + sz])
        off += sz
    q, kc, vc, ks, vs, kw, vw, g, mq = parts
    g = jnp.pad(g, ((0, 0), (0, LANES - g.shape[1])))
    packed = [_pad_groups(q, 1), _pad_groups(_rot_columns(q, NSA_HEADS), 1)]
    for k in (kc, ks, kw):
        packed += [k, _rot_columns(k, NSA_KV_HEADS)]
    packed += [vc, vs, vw, mq, g]
    return jnp.concatenate(packed, axis=1).astype(BF16)


def _rope_tables(seq):
    half = HEAD_DIM // 2
    pos = jnp.arange(seq, dtype=F32)
    inv = ROPE_THETA ** (-jnp.arange(half, dtype=F32) / half)
    ang = pos[:, None] * inv[None, :]
    reps = LANES // half
    return jnp.tile(jnp.cos(ang), (1, reps)), jnp.tile(jnp.sin(ang), (1, reps))


def _selection_map_t(seq):
    n_rows = seq // CMP_STRIDE
    n_blk = seq // SEL_BLOCK
    c0 = jnp.arange(n_rows) * CMP_STRIDE
    b0 = jnp.arange(n_blk) * SEL_BLOCK
    lo = jnp.maximum(c0[None, :], b0[:, None])
    hi = jnp.minimum(c0[None, :] + CMP_LEN, b0[:, None] + SEL_BLOCK)
    frac = jnp.clip(hi - lo, 0).astype(F32) / CMP_LEN
    return jnp.where(jnp.arange(n_rows)[None, :] < n_rows - 1, frac, 0.0).astype(BF16)


def _window_bands():
    back = jnp.arange(WINDOW // Q_STEP + 1)[:, None, None]
    k_loc = jnp.arange(WINDOW + Q_STEP)[None, :, None]
    t_loc = jnp.arange(Q_STEP)[None, None, :]
    rel = k_loc - back * Q_STEP - t_loc
    return jnp.where((rel <= 0) & (rel > -WINDOW), 0.0, NEG_INF).astype(F32)


def _nsa_layer_mixer(x, gain, w_in, pe_k, pe_v, ck_w1, ck_w2, cv_w1, cv_w2, batch, seq):
    cos, sin = _rope_tables(seq)
    q, kc, vc, ks, vs, kw, vw, mem_q, gate = _nsa_in_proj(
        x, gain, _pack_nsa_weight(w_in), cos, sin, batch, seq)

    n_rows = seq // CMP_STRIDE
    half = CMP_STRIDE * HEAD_DIM
    chunked = lambda t: t.reshape(batch, NSA_KV_HEADS, n_rows, half)
    pe2 = lambda pe: pe.reshape(2, half)
    kcmp, vcmp = _compress(
        chunked(kc), chunked(vc), pe2(pe_k), pe2(pe_v),
        ck_w1[:half].astype(BF16), ck_w1[half:].astype(BF16), ck_w2.astype(BF16),
        cv_w1[:half].astype(BF16), cv_w1[half:].astype(BF16), cv_w2.astype(BF16))

    n_q = seq // Q_STEP
    gates = gate[:, :N_BRANCH * NSA_HEADS].reshape(batch, n_q, Q_STEP, NSA_KV_HEADS, NSA_GROUP, N_BRANCH)
    gates = gates.transpose(0, 3, 1, 5, 4, 2).reshape(batch, NSA_KV_HEADS, n_q, N_BRANCH, Q_COLS)
    n_blk = seq // SEL_BLOCK
    ltri = (jnp.arange(n_blk)[None, :] < jnp.arange(n_blk)[:, None]).astype(BF16)
    mix = _nsa_attention(q, kcmp, vcmp, ks, vs, kw, vw, gates, _selection_map_t(seq), ltri, _window_bands())
    return mix, mem_q


def kernel(x, mem, norm_mix, norm_ffn, norm_mem, norm_final, w_mem_kv, ffn_w_gate, ffn_w_up, ffn_w_down,
           gmlp_w_in, gmlp_v_norm, gmlp_w_s, gmlp_b_s, gmlp_w_out,
           nsa_w_in, nsa_pe_k, nsa_pe_v, nsa_ck_w1, nsa_ck_w2, nsa_cv_w1, nsa_cv_w2, nsa_w_out):
    batch, seq, d = x.shape
    mem_len = mem.shape[1]
    depth = norm_mix.shape[0]
    assert seq % max(ROW_TILE, KEY_TILE) == 0 and seq >= WINDOW + Q_STEP and d == D_MODEL
    row = lambda v: v.reshape(1, -1)
    xt = x.reshape(batch * seq, d)
    mem_t = mem.reshape(batch * mem_len, d)
    for i in range(depth):
        j = i // 2
        mem_kv = _norm_matmul(mem_t, row(norm_mem[i]), w_mem_kv[i].astype(BF16), BF16)
        mem_kv = mem_kv.reshape(batch, mem_len, 2 * MEM_WIDTH)
        if i % 2 == 0:
            b_full = jnp.repeat(gmlp_b_s[j].T, GMLP_WIDTH // GMLP_GROUPS, axis=1)
            mix, mem_q = _gmlp_mixer(xt, row(norm_mix[i]), gmlp_w_in[j].astype(BF16), row(gmlp_v_norm[j]),
                                     gmlp_w_s[j].astype(BF16), b_full)
            w_out = gmlp_w_out[j]
        else:
            mix, mem_q = _nsa_layer_mixer(xt, row(norm_mix[i]), nsa_w_in[j], nsa_pe_k[j], nsa_pe_v[j],
                                          nsa_ck_w1[j], nsa_ck_w2[j], nsa_cv_w1[j], nsa_cv_w2[j], batch, seq)
            w_out = jnp.concatenate([_pad_groups(nsa_w_out[j][:NSA_WIDTH], 0), nsa_w_out[j][NSA_WIDTH:]], axis=0)
        xt = _post(xt, mix, mem_q, mem_kv, w_out.astype(BF16), row(norm_ffn[i]),
                   ffn_w_gate[i].astype(BF16), ffn_w_up[i].astype(BF16), ffn_w_down[i].astype(BF16),
                   row(norm_final), seq, final=(i == depth - 1))
    return xt.reshape(batch, seq, d)
```
